```python
import jax, jax.numpy as jnp
from jax import lax
import numpy as np

D_MODEL = 2048
BATCH = 2
SEQ = 4096
DEPTH = 4

HEAD_DIM = 128
MIX_WIDTH = D_MODEL
MOBA_HEADS = MIX_WIDTH // 2 // HEAD_DIM
MOBA_WIDTH = MOBA_HEADS * HEAD_DIM
MOBA_BLOCK = 256
MOBA_TOPK = 3
MOBA_Q_CHUNK = 32
DN_HEADS = MIX_WIDTH // 2 // HEAD_DIM
DN_DK = HEAD_DIM
DN_DV = HEAD_DIM
DN_QK_WIDTH = DN_HEADS * DN_DK
DN_WIDTH = DN_HEADS * DN_DV
DN_CONV_CH = 2 * DN_QK_WIDTH + DN_WIDTH
CONV_W = 4
DN_CHUNK = 64
AB_WIDTH = MOBA_WIDTH + DN_WIDTH
AB_IN = 3 * MOBA_WIDTH + DN_CONV_CH + DN_WIDTH + 2 * DN_HEADS
AB_SPLITS = (3 * MOBA_WIDTH, 3 * MOBA_WIDTH + DN_CONV_CH, 3 * MOBA_WIDTH + DN_CONV_CH + DN_WIDTH, 3 * MOBA_WIDTH + DN_CONV_CH + DN_WIDTH + DN_HEADS)
C_DK = 128
C_HEADS = MIX_WIDTH // C_DK
C_DV = MIX_WIDTH // C_HEADS
C_WIDTH = MIX_WIDTH
C_CHUNK = 64
MEM_TOKENS = 256
X_HEADS = 4
X_DIM = 128
X_WIDTH = X_HEADS * X_DIM
D_FF = 5504
N_EVEN = (DEPTH + 1) // 2
N_ODD = DEPTH // 2
EPS = 1e-6
F32 = jnp.float32

kernel_name = 'hybrid_moba_gdn_hgrn2_macaron'


def rmsnorm(x, g):
    xf = x.astype(F32)
    y = xf * lax.rsqrt(jnp.mean(xf * xf, axis=-1, keepdims=True) + EPS)
    return (y * g.astype(F32)).astype(x.dtype)


def l2norm(x):
    return x * lax.rsqrt(jnp.sum(x * x, axis=-1, keepdims=True) + EPS)


def swiglu(h, w_in, w_out):
    a, b = jnp.split(h @ w_in, 2, axis=-1)
    return (jax.nn.silu(a) * b) @ w_out


def causal_dwconv(x, w):
    width, ch = w.shape
    return lax.conv_general_dilated(x, w.astype(x.dtype)[:, None, :], window_strides=(1,), padding=[(width - 1, 0)], dimension_numbers=('NWC', 'WIO', 'NWC'), feature_group_count=ch)


def moba_attention(q, k, v):
    B, T, H, dh = q.shape
    nb = -(-T // MOBA_BLOCK)
    t_pad = nb * MOBA_BLOCK
    pad = ((0, 0), (0, t_pad - T), (0, 0), (0, 0))
    kh = jnp.pad(k, pad).transpose(0, 2, 1, 3).reshape(B, H, nb, MOBA_BLOCK, dh)
    vh = jnp.pad(v, pad).transpose(0, 2, 1, 3).reshape(B, H, nb, MOBA_BLOCK, dh)
    qh = q.transpose(0, 2, 1, 3)
    k_mean = jnp.mean(kh.astype(F32), axis=3)
    gate = jnp.einsum('bhtd,bhnd->bhtn', qh.astype(F32), k_mean)
    pos = jnp.arange(T)
    q_blk = pos // MOBA_BLOCK
    past = jnp.arange(nb)[None, :] < q_blk[:, None]
    gate = jnp.where(past, gate, -jnp.inf)
    kk = min(MOBA_TOPK, nb)
    _, top = lax.top_k(gate, kk)
    own = jnp.broadcast_to(q_blk[:, None], (B, H, T, 1)).astype(top.dtype)
    idx = jnp.concatenate([top, own], axis=-1)
    slot_ok = jnp.concatenate([top < q_blk[:, None], jnp.ones((B, H, T, 1), bool)], axis=-1)
    nq = T // MOBA_Q_CHUNK

    def to_q_chunks(a):
        a = a.reshape(B, H, nq, MOBA_Q_CHUNK, *a.shape[3:])
        return jnp.moveaxis(a, 2, 0)

    b_ix = jnp.arange(B)[:, None, None, None]
    h_ix = jnp.arange(H)[None, :, None, None]
    blk_off = jnp.arange(MOBA_BLOCK)
    scale = dh ** -0.5

    def attend(args):
        qc, ic, okc, pc = args
        ks = kh[b_ix, h_ix, ic]
        vs = vh[b_ix, h_ix, ic]
        s = jnp.einsum('bhqd,bhqnkd->bhqnk', qc, ks).astype(F32) * scale
        key_pos = ic[..., None] * MOBA_BLOCK + blk_off
        mask = okc[..., None] & (key_pos <= pc[:, None, None])
        s = jnp.where(mask, s, -jnp.inf)
        p = jax.nn.softmax(s.reshape(*s.shape[:3], -1), axis=-1).reshape(s.shape)
        return jnp.einsum('bhqnk,bhqnkd->bhqd', p.astype(vs.dtype), vs)

    out = lax.map(attend, (to_q_chunks(qh), to_q_chunks(idx), to_q_chunks(slot_ok), pos.reshape(nq, MOBA_Q_CHUNK)))
    out = jnp.moveaxis(out, 0, 2).reshape(B, H, T, dh).transpose(0, 2, 1, 3)
    return out.reshape(B, T, H * dh)


def gated_delta_rule(q, k, v, g, beta):
    B, T, H, dk = q.shape
    dv = v.shape[-1]
    C = DN_CHUNK
    n = T // C

    def chunks(a):
        a = a.reshape(B, n, C, H, *a.shape[3:])
        return jnp.moveaxis(a, 3, 1)

    q, k, v, g, beta = chunks(q), chunks(k), chunks(v), chunks(g), chunks(beta)
    G = jnp.cumsum(g, axis=-1)
    tril = jnp.tril(jnp.ones((C, C), bool))
    eye = jnp.eye(C, dtype=F32)
    gamma = jnp.exp(jnp.where(tril, G[..., :, None] - G[..., None, :], -jnp.inf))
    kb = k * beta[..., None]
    m = jnp.where(tril & (eye == 0), jnp.einsum('bhncd,bhnsd->bhncs', kb, k) * gamma, 0.0)
    t_inv = lax.linalg.triangular_solve(m + eye, jnp.broadcast_to(eye, m.shape), left_side=True, lower=True, unit_diagonal=True)
    u = t_inv @ (v * beta[..., None])
    w = t_inv @ (kb * jnp.exp(G)[..., None])
    a_qk = jnp.einsum('bhncd,bhnsd->bhncs', q, k) * gamma
    q_dec = q * jnp.exp(G)[..., None]
    k_dec = k * jnp.exp(G[..., -1:] - G)[..., None]
    g_last = jnp.exp(G[..., -1])
    xs = tuple(jnp.moveaxis(a, 2, 0) for a in (u, w, a_qk, q_dec, k_dec, g_last))

    def step(S, inp):
        u_c, w_c, a_c, qd, kd, gl = inp
        v_new = u_c - w_c @ S
        o = qd @ S + a_c @ v_new
        S = gl[..., None, None] * S + jnp.einsum('bhcd,bhcv->bhdv', kd, v_new)
        return S, o

    _, o = lax.scan(step, jnp.zeros((B, H, dk, dv), F32), xs)
    return o.transpose(1, 0, 3, 2, 4).reshape(B, T, H, dv)


def gla_chunked(q, k, v, log_f):
    B, T, H, dk = q.shape
    dv = v.shape[-1]
    C = C_CHUNK
    n = T // C

    def chunks(a):
        a = a.reshape(B, n, C, H, a.shape[-1])
        return a.transpose(1, 0, 3, 2, 4)

    qc, kc, vc = chunks(q), chunks(k), chunks(v)
    bc = jnp.cumsum(chunks(log_f), axis=3)
    tril = jnp.tril(jnp.ones((C, C), bool))[:, :, None]

    def step(S, inp):
        qi, ki, vi, bi = inp
        decay = jnp.exp(jnp.where(tril, bi[:, :, :, None, :] - bi[:, :, None, :, :], -jnp.inf))
        a = jnp.einsum('bhtd,bhsd,bhtsd->bhts', qi, ki, decay)
        o = a @ vi + (qi * jnp.exp(bi)) @ S
        b_last = bi[:, :, -1:, :]
        S = jnp.exp(b_last[:, :, 0, :])[..., None] * S + jnp.einsum('bhsd,bhsv->bhdv', ki * jnp.exp(b_last - bi), vi)
        return S, o

    _, o = lax.scan(step, jnp.zeros((B, H, dk, dv), F32), (qc, kc, vc, bc))
    return o.transpose(1, 0, 3, 2, 4).reshape(B, T, H, dv)


def moba_deltanet_mixer(h, w_in, conv_w, a_log, dt_bias, o_norm_g, w_out):
    B, T, _ = h.shape
    moba_qkv, dn_qkv, dn_gate, beta_raw, alpha_raw = jnp.split(h @ w_in, AB_SPLITS, axis=-1)
    mq, mk, mv = (a.reshape(B, T, MOBA_HEADS, HEAD_DIM) for a in jnp.split(moba_qkv, 3, axis=-1))
    y_a = moba_attention(mq, mk, mv)
    dn = jax.nn.silu(causal_dwconv(dn_qkv, conv_w)).astype(F32)
    dq, dk, dv = jnp.split(dn, (DN_QK_WIDTH, 2 * DN_QK_WIDTH), axis=-1)
    dq = l2norm(dq.reshape(B, T, DN_HEADS, DN_DK)) * DN_DK ** -0.5
    dk = l2norm(dk.reshape(B, T, DN_HEADS, DN_DK))
    dv = dv.reshape(B, T, DN_HEADS, DN_DV)
    beta = jax.nn.sigmoid(beta_raw.astype(F32))
    g = -jnp.exp(a_log.astype(F32)) * jax.nn.softplus(alpha_raw.astype(F32) + dt_bias.astype(F32))
    o = gated_delta_rule(dq, dk, dv, g, beta)
    y_b = rmsnorm(o, o_norm_g) * jax.nn.silu(dn_gate.astype(F32).reshape(B, T, DN_HEADS, DN_DV))
    y = jnp.concatenate([y_a, y_b.reshape(B, T, DN_WIDTH).astype(h.dtype)], axis=-1)
    return y @ w_out


def hgrn2_mixer(h, w_in, lb, o_norm_g, w_out):
    B, T, _ = h.shape
    q_raw, f_raw, i_raw, g_raw = jnp.split(h @ w_in, 4, axis=-1)
    f_raw = f_raw.astype(F32)
    lb = lb.astype(F32)
    log_f = jnp.log(lb + (1.0 - lb) * jax.nn.sigmoid(f_raw))
    k = (1.0 - lb) * jax.nn.sigmoid(-f_raw)
    q = jax.nn.silu(q_raw.astype(F32)) * C_DK ** -0.5
    heads = lambda a, d: a.reshape(B, T, C_HEADS, d)
    o = gla_chunked(heads(q, C_DK), heads(k, C_DK), heads(i_raw.astype(F32), C_DV), heads(log_f, C_DK))
    o = rmsnorm(o, o_norm_g) * jax.nn.silu(heads(g_raw.astype(F32), C_DV))
    return o.reshape(B, T, C_WIDTH).astype(h.dtype) @ w_out


def mem_cross_attention(h, mem_n, w_q, w_kv, w_o):
    B, T, _ = h.shape
    M = mem_n.shape[1]
    q = (h @ w_q).reshape(B, T, X_HEADS, X_DIM)
    k, v = jnp.split(mem_n @ w_kv, 2, axis=-1)
    k = k.reshape(B, M, X_HEADS, X_DIM)
    v = v.reshape(B, M, X_HEADS, X_DIM)
    s = jnp.einsum('bthd,bmhd->bhtm', q, k).astype(F32) * X_DIM ** -0.5
    p = jax.nn.softmax(s, axis=-1).astype(v.dtype)
    o = jnp.einsum('bhtm,bmhd->bthd', p, v).reshape(B, T, X_WIDTH)
    return o @ w_o


def setup_inputs(seed: int = 0) -> dict:
    key = jax.random.key(seed)
    ks = jax.random.split(key, 20)

    def nrm(k, shape, fan_in, mult=1.0):
        return jax.random.normal(k, shape, F32) * (mult * fan_in ** -0.5)

    def gain(k, shape):
        return 1.0 + 0.02 * jax.random.normal(k, shape, F32)

    dt = jnp.exp(jax.random.uniform(ks[9], (N_EVEN, DN_HEADS), F32, float(np.log(1e-3)), float(np.log(1e-1))))
    return {
        'x': jax.random.normal(ks[0], (BATCH, SEQ, D_MODEL), F32),
        'mem': jax.random.normal(ks[1], (BATCH, MEM_TOKENS, D_MODEL), F32),
        'norm_g': gain(ks[2], (DEPTH, 4, D_MODEL)),
        'mem_norm_g': gain(ks[3], (D_MODEL,)),
        'final_norm_g': gain(ks[4], (D_MODEL,)),
        'ffn_w_in': nrm(ks[5], (DEPTH, 2, D_MODEL, 2 * D_FF), D_MODEL),
        'ffn_w_out': nrm(ks[6], (DEPTH, 2, D_FF, D_MODEL), D_FF, 0.5),
        'ab_w_in': nrm(ks[7], (N_EVEN, D_MODEL, AB_IN), D_MODEL),
        'ab_conv_w': nrm(ks[8], (N_EVEN, CONV_W, DN_CONV_CH), CONV_W),
        'ab_a_log': jnp.log(jax.random.uniform(ks[10], (N_EVEN, DN_HEADS), F32, 1.0, 16.0)),
        'ab_dt_bias': dt + jnp.log(-jnp.expm1(-dt)),
        'ab_o_norm_g': gain(ks[11], (N_EVEN, DN_DV)),
        'ab_w_out': nrm(ks[12], (N_EVEN, AB_WIDTH, D_MODEL), AB_WIDTH, 0.5),
        'c_w_in': nrm(ks[13], (N_ODD, D_MODEL, 4 * C_WIDTH), D_MODEL),
        'c_lb_logits': 0.1 * jax.random.normal(ks[14], (DEPTH, C_WIDTH), F32),
        'c_o_norm_g': gain(ks[15], (N_ODD, C_DV)),
        'c_w_out': nrm(ks[16], (N_ODD, C_WIDTH, D_MODEL), C_WIDTH, 0.5),
        'x_w_q': nrm(ks[17], (DEPTH, D_MODEL, X_WIDTH), D_MODEL),
        'x_w_kv': nrm(ks[18], (DEPTH, D_MODEL, 2 * X_WIDTH), D_MODEL),
        'x_w_o': nrm(ks[19], (DEPTH, X_WIDTH, D_MODEL), X_WIDTH, 0.5),
    }


def reference(x, mem, norm_g, mem_norm_g, final_norm_g, ffn_w_in, ffn_w_out, ab_w_in, ab_conv_w, ab_a_log, ab_dt_bias, ab_o_norm_g, ab_w_out, c_w_in, c_lb_logits, c_o_norm_g, c_w_out, x_w_q, x_w_kv, x_w_o):
    p_lb = jax.nn.softmax(c_lb_logits.astype(F32), axis=0)
    lb_all = jnp.cumsum(p_lb, axis=0) - p_lb[0]
    mem_n = rmsnorm(mem, mem_norm_g)
    for l in range(DEPTH):
        x = x + 0.5 * swiglu(rmsnorm(x, norm_g[l, 0]), ffn_w_in[l, 0], ffn_w_out[l, 0])
        h = rmsnorm(x, norm_g[l, 1])
        if l % 2 == 0:
            e = l // 2
            x = x + moba_deltanet_mixer(h, ab_w_in[e], ab_conv_w[e], ab_a_log[e], ab_dt_bias[e], ab_o_norm_g[e], ab_w_out[e])
        else:
            o = l // 2
            x = x + hgrn2_mixer(h, c_w_in[o], lb_all[l], c_o_norm_g[o], c_w_out[o])
        x = x + mem_cross_attention(rmsnorm(x, norm_g[l, 2]), mem_n, x_w_q[l], x_w_kv[l], x_w_o[l])
        x = x + 0.5 * swiglu(rmsnorm(x, norm_g[l, 3]), ffn_w_in[l, 1], ffn_w_out[l, 1])
    return rmsnorm(x, final_norm_g)
```

```python
import functools

import jax
import jax.numpy as jnp
from jax import lax
from jax.experimental import pallas as pl
from jax.experimental.pallas import tpu as pltpu

F32 = jnp.float32
MXU_DTYPE = jnp.bfloat16
HIGHEST = lax.Precision.HIGHEST
EPS = 1e-6
NEG_INF = float("-inf")

HEAD_DIM = 128
MOBA_BLOCK = 256
MOBA_TOPK = 3
CHUNK = 64
SUB = 16
CONV_W = 4
CONV_HALO = 8
X_HEADS = 4
VMEM_LIMIT = 56 * 1024 * 1024


def _cparams(*sem):
    return pltpu.CompilerParams(dimension_semantics=sem, vmem_limit_bytes=VMEM_LIMIT)


def _tile(n, t):
    if n <= t:
        return n
    t -= t % HEAD_DIM
    while n % t:
        t -= HEAD_DIM
    return t


def _mm(a, b):
    return jnp.dot(a.astype(MXU_DTYPE), b.astype(MXU_DTYPE), preferred_element_type=F32)


def _mm_nt(a, b):
    return lax.dot_general(a.astype(MXU_DTYPE), b.astype(MXU_DTYPE), (((1,), (1,)), ((), ())),
                           preferred_element_type=F32)


def _mm_tn(a, b):
    return lax.dot_general(a.astype(MXU_DTYPE), b.astype(MXU_DTYPE), (((0,), (0,)), ((), ())),
                           preferred_element_type=F32)


def _mm_f32(a, b):
    return jnp.dot(a, b, precision=HIGHEST, preferred_element_type=F32)


def _sigmoid(x):
    return 1.0 / (1.0 + jnp.exp(-x))


def _silu(x):
    return x * _sigmoid(x)


def _rms(x, g):
    return x * lax.rsqrt(jnp.mean(x * x, axis=-1, keepdims=True) + EPS) * g


def _norm_matmul_kernel(x_ref, g_ref, w_ref, o_ref, xn_ref):
    @pl.when(pl.program_id(1) == 0)
    def _():
        xn_ref[...] = _rms(x_ref[...], g_ref[...]).astype(xn_ref.dtype)

    o_ref[...] = jnp.dot(xn_ref[...], w_ref[...], preferred_element_type=F32)


def _norm_matmul(x, g, w, *, tm, tn):
    m, k = x.shape
    n = w.shape[1]
    tm, tn = _tile(m, tm), _tile(n, tn)
    return pl.pallas_call(
        _norm_matmul_kernel,
        grid=(m // tm, n // tn),
        in_specs=[pl.BlockSpec((tm, k), lambda i, j: (i, 0)),
                  pl.BlockSpec((1, k), lambda i, j: (0, 0)),
                  pl.BlockSpec((k, tn), lambda i, j: (0, j))],
        out_specs=pl.BlockSpec((tm, tn), lambda i, j: (i, j)),
        out_shape=jax.ShapeDtypeStruct((m, n), F32),
        scratch_shapes=[pltpu.VMEM((tm, k), MXU_DTYPE)],
        compiler_params=_cparams("parallel", "arbitrary"),
        name="norm_matmul",
    )(x, g.reshape(1, k), w)


def _ffn_kernel(x_ref, g_ref, wa_ref, wb_ref, wo_ref, o_ref, xn_ref, acc_ref):
    j = pl.program_id(1)

    @pl.when(j == 0)
    def _():
        xn_ref[...] = _rms(x_ref[...], g_ref[...]).astype(xn_ref.dtype)
        acc_ref[...] = jnp.zeros_like(acc_ref)

    xn = xn_ref[...]
    a = jnp.dot(xn, wa_ref[...], preferred_element_type=F32)
    b = jnp.dot(xn, wb_ref[...], preferred_element_type=F32)
    acc_ref[...] += _mm(_silu(a) * b, wo_ref[...])

    @pl.when(j == pl.num_programs(1) - 1)
    def _():
        o_ref[...] = x_ref[...] + 0.5 * acc_ref[...]


def _ffn(x, g, wa, wb, wo, *, tm, tf):
    m, d = x.shape
    f = wa.shape[1]
    tm = min(tm, m)
    assert m % tm == 0 and f % tf == 0
    return pl.pallas_call(
        _ffn_kernel,
        grid=(m // tm, f // tf),
        in_specs=[pl.BlockSpec((tm, d), lambda i, j: (i, 0)),
                  pl.BlockSpec((1, d), lambda i, j: (0, 0)),
                  pl.BlockSpec((d, tf), lambda i, j: (0, j)),
                  pl.BlockSpec((d, tf), lambda i, j: (0, j)),
                  pl.BlockSpec((tf, d), lambda i, j: (j, 0))],
        out_specs=pl.BlockSpec((tm, d), lambda i, j: (i, 0)),
        out_shape=jax.ShapeDtypeStruct((m, d), F32),
        scratch_shapes=[pltpu.VMEM((tm, d), MXU_DTYPE), pltpu.VMEM((tm, d), F32)],
        compiler_params=_cparams("parallel", "arbitrary"),
        name="ffn",
    )(x, g.reshape(1, d), wa, wb, wo)


def _ffn_weights(w_in, w_out, tf):
    f = w_out.shape[0]
    pad = (-f) % tf
    wa = jnp.pad(w_in[:, :f], ((0, 0), (0, pad))).astype(MXU_DTYPE)
    wb = jnp.pad(w_in[:, f:], ((0, 0), (0, pad))).astype(MXU_DTYPE)
    wo = jnp.pad(w_out, ((0, pad), (0, 0))).astype(MXU_DTYPE)
    return wa, wb, wo


def _matmul_residual_kernel(*refs, n_in):
    x_ref, o_ref = refs[0], refs[-1]
    acc = x_ref[...]
    for i in range(n_in):
        acc = acc + _mm(refs[1 + i][...], refs[1 + n_in + i][...])
    o_ref[...] = acc


def _matmul_residual(x, ys, ws, *, tm, tn):
    m, n = x.shape
    tm, tn = _tile(m, tm), _tile(n, tn)
    n_in = len(ys)
    in_specs = [pl.BlockSpec((tm, tn), lambda i, j: (i, j))]
    in_specs += [pl.BlockSpec((tm, y.shape[1]), lambda i, j: (i, 0)) for y in ys]
    in_specs += [pl.BlockSpec((w.shape[0], tn), lambda i, j: (0, j)) for w in ws]
    return pl.pallas_call(
        functools.partial(_matmul_residual_kernel, n_in=n_in),
        grid=(m // tm, n // tn),
        in_specs=in_specs,
        out_specs=pl.BlockSpec((tm, tn), lambda i, j: (i, j)),
        out_shape=jax.ShapeDtypeStruct((m, n), F32),
        compiler_params=_cparams("parallel", "arbitrary"),
        name="matmul_residual",
    )(x, *ys, *ws)


def _moba_kernel(q_ref, k_ref, v_ref, o_ref, kb_ref, vb_ref, kmean_ref, *, seq):
    qi = pl.program_id(2)
    nb = seq // MOBA_BLOCK
    blk_rows = MOBA_BLOCK

    @pl.when(qi == 0)
    def _():
        k = k_ref[...]
        kb_ref[...] = k.astype(kb_ref.dtype)
        vb_ref[...] = v_ref[...].astype(vb_ref.dtype)
        kmean_ref[...] = jnp.mean(k.reshape(nb, blk_rows, HEAD_DIM), axis=1)

    q = q_ref[...]
    gate = lax.dot_general(q, kmean_ref[...], (((1,), (1,)), ((), ())), precision=HIGHEST,
                           preferred_element_type=F32)
    blk = lax.broadcasted_iota(jnp.int32, (blk_rows, nb), 1)
    past = blk < qi
    gate = jnp.where(past, gate, NEG_INF)
    rank = jnp.zeros((blk_rows, nb), jnp.int32)
    for m in range(nb):
        gm = gate[:, m:m + 1]
        beats = (gm > gate) | ((gm == gate) & (m < blk))
        rank = rank + beats.astype(jnp.int32)
    sel = jnp.where(past & (rank < MOBA_TOPK), 1.0, 0.0)

    scale = HEAD_DIM ** -0.5
    qb = q.astype(kb_ref.dtype)
    own = pl.multiple_of(qi * blk_rows, blk_rows)
    row = lax.broadcasted_iota(jnp.int32, (blk_rows, blk_rows), 0)
    col = lax.broadcasted_iota(jnp.int32, (blk_rows, blk_rows), 1)
    s = _mm_nt(qb, kb_ref[pl.ds(own, blk_rows), :]) * scale
    s = jnp.where(row >= col, s, NEG_INF)
    m0 = jnp.max(s, axis=1, keepdims=True)
    p = jnp.exp(s - m0)
    l0 = jnp.sum(p, axis=1, keepdims=True)
    acc0 = _mm(p, vb_ref[pl.ds(own, blk_rows), :])

    def body(n, carry):
        m_run, l_run, acc = carry
        start = pl.multiple_of(n * blk_rows, blk_rows)
        s = _mm_nt(qb, kb_ref[pl.ds(start, blk_rows), :]) * scale
        chosen = jnp.sum(jnp.where(blk == n, sel, 0.0), axis=1, keepdims=True) > 0.0
        s = jnp.where(chosen, s, NEG_INF)
        m_new = jnp.maximum(m_run, jnp.max(s, axis=1, keepdims=True))
        alpha = jnp.exp(m_run - m_new)
        p = jnp.exp(s - m_new)
        l_new = alpha * l_run + jnp.sum(p, axis=1, keepdims=True)
        acc = alpha * acc + _mm(p, vb_ref[pl.ds(start, blk_rows), :])
        return m_new, l_new, acc

    _, l_fin, acc = lax.fori_loop(0, qi, body, (m0, l0, acc0))
    o_ref[...] = acc / l_fin


def _moba(proj, *, heads):
    b, t, _ = proj.shape
    nq = t // MOBA_BLOCK
    assert t % MOBA_BLOCK == 0
    return pl.pallas_call(
        functools.partial(_moba_kernel, seq=t),
        grid=(b, heads, nq),
        in_specs=[pl.BlockSpec((None, MOBA_BLOCK, HEAD_DIM), lambda bi, h, i: (bi, i, h)),
                  pl.BlockSpec((None, t, HEAD_DIM), lambda bi, h, i: (bi, 0, heads + h)),
                  pl.BlockSpec((None, t, HEAD_DIM), lambda bi, h, i: (bi, 0, 2 * heads + h))],
        out_specs=pl.BlockSpec((None, MOBA_BLOCK, HEAD_DIM), lambda bi, h, i: (bi, i, h)),
        out_shape=jax.ShapeDtypeStruct((b, t, heads * HEAD_DIM), F32),
        scratch_shapes=[pltpu.VMEM((t, HEAD_DIM), MXU_DTYPE), pltpu.VMEM((t, HEAD_DIM), MXU_DTYPE),
                        pltpu.VMEM((t // MOBA_BLOCK, HEAD_DIM), F32)],
        compiler_params=_cparams("parallel", "parallel", "arbitrary"),
        name="moba",
    )(proj, proj, proj)


def _chunk_masks():
    row = lax.broadcasted_iota(jnp.int32, (CHUNK, CHUNK), 0)
    col = lax.broadcasted_iota(jnp.int32, (CHUNK, CHUNK), 1)
    return row >= col, row > col, (row == col).astype(F32)


def _conv_silu(prev, x, cw):
    xe = jnp.concatenate([prev, x], axis=0)
    acc = x * cw[CONV_W - 1:CONV_W, :]
    for k in range(1, CONV_W):
        acc = acc + pltpu.roll(xe, k, axis=0)[CONV_HALO:, :] * cw[CONV_W - 1 - k:CONV_W - k, :]
    return _silu(acc)


def _deltanet_kernel(q_ref, k_ref, v_ref, gate_ref, small_ref, cwq_ref, cwk_ref, cwv_ref,
                     alog_ref, dt_ref, ong_ref, o_ref, *, heads, seq):
    h = pl.program_id(1)
    tril, strict, eye = _chunk_masks()
    ones_tril = tril.astype(F32)
    lane = lax.broadcasted_iota(jnp.int32, (CHUNK, HEAD_DIM), 1)
    neg_a = -jnp.exp(alog_ref[pl.ds(h, 1), :])
    dt_row = dt_ref[pl.ds(h, 1), :]
    cwq, cwk, cwv = cwq_ref[...], cwk_ref[...], cwv_ref[...]
    ong = ong_ref[...]

    def body(c, carry):
        state, prev_q, prev_k, prev_v = carry
        r0 = pl.multiple_of(c * CHUNK, CHUNK)
        xq, xk, xv = q_ref[pl.ds(r0, CHUNK), :], k_ref[pl.ds(r0, CHUNK), :], v_ref[pl.ds(r0, CHUNK), :]
        dq = _conv_silu(prev_q, xq, cwq)
        dk = _conv_silu(prev_k, xk, cwk)
        dv = _conv_silu(prev_v, xv, cwv)
        dq = dq * lax.rsqrt(jnp.sum(dq * dq, axis=-1, keepdims=True) + EPS) * (HEAD_DIM ** -0.5)
        dk = dk * lax.rsqrt(jnp.sum(dk * dk, axis=-1, keepdims=True) + EPS)
        sm = small_ref[pl.ds(r0, CHUNK), :]
        beta_raw = jnp.sum(jnp.where(lane == h, sm, 0.0), axis=1, keepdims=True)
        alpha_raw = jnp.sum(jnp.where(lane == heads + h, sm, 0.0), axis=1, keepdims=True)
        beta = _sigmoid(beta_raw)
        z = alpha_raw + dt_row
        g = neg_a * (jnp.maximum(z, 0.0) + jnp.log1p(jnp.exp(-jnp.abs(z))))
        gc = _mm_f32(ones_tril, g)
        g_col = gc[:, :CHUNK]
        g_row = jnp.sum(g_col * eye, axis=0, keepdims=True)
        gamma = jnp.exp(jnp.where(tril, g_col - g_row, NEG_INF))
        e_g = jnp.exp(gc)
        g_last = gc[CHUNK - 1:CHUNK, :]
        kb = dk * beta
        m_mat = jnp.where(strict, _mm_nt(kb, dk) * gamma, 0.0)
        npow = -m_mat
        t_inv = eye + npow
        for _ in range(CHUNK.bit_length() - 2):
            npow = _mm_f32(npow, npow)
            t_inv = t_inv + _mm_f32(t_inv, npow)
        u = _mm(t_inv, dv * beta)
        w = _mm(t_inv, kb * e_g)
        a_qk = _mm_nt(dq, dk) * gamma
        v_new = u - _mm(w, state)
        o = _mm(dq * e_g, state) + _mm(a_qk, v_new)
        state = jnp.exp(g_last) * state + _mm_tn(dk * jnp.exp(g_last - gc), v_new)
        gate = gate_ref[pl.ds(r0, CHUNK), :]
        o_ref[pl.ds(r0, CHUNK), :] = _rms(o, ong) * _silu(gate)
        return (state, xq[CHUNK - CONV_HALO:, :], xk[CHUNK - CONV_HALO:, :], xv[CHUNK - CONV_HALO:, :])

    halo = jnp.zeros((CONV_HALO, HEAD_DIM), F32)
    lax.fori_loop(0, seq // CHUNK, body, (jnp.zeros((HEAD_DIM, HEAD_DIM), F32), halo, halo, halo))


def _deltanet(proj, small, conv_w, a_log, dt_bias, o_norm_g, *, col0, heads):
    b, t, _ = proj.shape
    assert t % CHUNK == 0

    def col(off):
        return pl.BlockSpec((None, t, HEAD_DIM), lambda bi, h: (bi, 0, col0 + off * heads + h))

    def cw(off):
        return pl.BlockSpec((CONV_W, HEAD_DIM), lambda bi, h: (0, off * heads + h))

    rep = lambda a: jnp.broadcast_to(a.astype(F32)[:, None], (heads, HEAD_DIM))
    whole = lambda r: pl.BlockSpec((r, HEAD_DIM), lambda bi, h: (0, 0))
    return pl.pallas_call(
        functools.partial(_deltanet_kernel, heads=heads, seq=t),
        grid=(b, heads),
        in_specs=[col(0), col(1), col(2), col(3),
                  pl.BlockSpec((None, t, HEAD_DIM), lambda bi, h: (bi, 0, 0)),
                  cw(0), cw(1), cw(2), whole(heads), whole(heads), whole(1)],
        out_specs=pl.BlockSpec((None, t, HEAD_DIM), lambda bi, h: (bi, 0, h)),
        out_shape=jax.ShapeDtypeStruct((b, t, heads * HEAD_DIM), F32),
        compiler_params=_cparams("parallel", "arbitrary"),
        name="deltanet",
    )(proj, proj, proj, proj, small, conv_w, conv_w, conv_w, rep(a_log), rep(dt_bias),
      o_norm_g.reshape(1, HEAD_DIM))


def _hgrn2_kernel(q_ref, f_ref, i_ref, g_ref, lbl_ref, ong_ref, o_ref, *, layer, seq):
    tril, _, _ = _chunk_masks()
    ones_tril = tril.astype(F32)
    lbl = lbl_ref[...]
    e = jnp.exp(lbl - jnp.max(lbl, axis=0, keepdims=True))
    p = e / jnp.sum(e, axis=0, keepdims=True)
    cs = p[0:1, :]
    for r in range(1, layer + 1):
        cs = cs + p[r:r + 1, :]
    lb = cs - p[0:1, :]
    ong = ong_ref[...]
    pos = lax.broadcasted_iota(jnp.int32, (CHUNK, 1), 0) % SUB

    def body(c, state_t):
        r0 = pl.multiple_of(c * CHUNK, CHUNK)
        fr = f_ref[pl.ds(r0, CHUNK), :]
        log_f = jnp.log(lb + (1.0 - lb) * _sigmoid(fr))
        k = (1.0 - lb) * _sigmoid(-fr)
        q = _silu(q_ref[pl.ds(r0, CHUNK), :]) * (HEAD_DIM ** -0.5)
        v = i_ref[pl.ds(r0, CHUNK), :]
        b = _mm_f32(ones_tril, log_f)
        b_last = b[CHUNK - 1:CHUNK, :]
        o = _mm_nt(q * jnp.exp(b), state_t)
        parts = [jnp.zeros((SUB, HEAD_DIM), F32)]
        for i in range(1, CHUNK // SUB):
            lo = i * SUB
            ref_row = b[lo:lo + 1, :]
            qt = q[lo:lo + SUB, :] * jnp.exp(b[lo:lo + SUB, :] - ref_row)
            kt = k[:lo, :] * jnp.exp(ref_row - b[:lo, :])
            parts.append(_mm(_mm_nt(qt, kt), v[:lo, :]))
        o = o + jnp.concatenate(parts, axis=0)
        for d in range(SUB):
            kr, br, vr = (k, b, v) if d == 0 else tuple(pltpu.roll(a, d, axis=0) for a in (k, b, v))
            decay = jnp.exp(jnp.where(pos >= d, b - br, NEG_INF))
            o = o + jnp.sum(q * kr * decay, axis=1, keepdims=True) * vr
        state_t = state_t * jnp.exp(b_last) + _mm_tn(v, k * jnp.exp(b_last - b))
        o_ref[pl.ds(r0, CHUNK), :] = _rms(o, ong) * _silu(g_ref[pl.ds(r0, CHUNK), :])
        return state_t

    lax.fori_loop(0, seq // CHUNK, body, jnp.zeros((HEAD_DIM, HEAD_DIM), F32))


def _hgrn2(proj, lb_logits, o_norm_g, *, layer, heads):
    b, t, _ = proj.shape
    depth = lb_logits.shape[0]
    assert t % CHUNK == 0

    def col(off):
        return pl.BlockSpec((None, t, HEAD_DIM), lambda bi, h: (bi, 0, off * heads + h))

    return pl.pallas_call(
        functools.partial(_hgrn2_kernel, layer=layer, seq=t),
        grid=(b, heads),
        in_specs=[col(0), col(1), col(2), col(3),
                  pl.BlockSpec((depth, HEAD_DIM), lambda bi, h: (0, h)),
                  pl.BlockSpec((1, HEAD_DIM), lambda bi, h: (0, 0))],
        out_specs=pl.BlockSpec((None, t, HEAD_DIM), lambda bi, h: (bi, 0, h)),
        out_shape=jax.ShapeDtypeStruct((b, t, heads * HEAD_DIM), F32),
        compiler_params=_cparams("parallel", "arbitrary"),
        name="hgrn2",
    )(proj, proj, proj, proj, lb_logits.astype(F32), o_norm_g.reshape(1, HEAD_DIM))


def _cross_attn_kernel(x_ref, g_ref, wq_ref, kv_ref, wo_ref, o_ref):
    x = x_ref[...]
    xn = _rms(x, g_ref[...])
    q = _mm(xn, wq_ref[...])
    width = X_HEADS * HEAD_DIM
    scale = HEAD_DIM ** -0.5
    outs = []
    for h in range(X_HEADS):
        lo = h * HEAD_DIM
        s = _mm_nt(q[:, lo:lo + HEAD_DIM], kv_ref[:, lo:lo + HEAD_DIM]) * scale
        p = jnp.exp(s - jnp.max(s, axis=-1, keepdims=True))
        p = p / jnp.sum(p, axis=-1, keepdims=True)
        outs.append(_mm(p, kv_ref[:, width + lo:width + lo + HEAD_DIM]))
    o_ref[...] = x + _mm(jnp.concatenate(outs, axis=-1), wo_ref[...])


def _cross_attn(x, g, wq, kv, wo, *, layer, seq, tm):
    m, d = x.shape
    mem_tokens = kv.shape[0] // (m // seq)
    tm = min(tm, seq)
    assert seq % tm == 0
    per_seq = seq // tm
    width = X_HEADS * HEAD_DIM
    return pl.pallas_call(
        _cross_attn_kernel,
        grid=(m // tm,),
        in_specs=[pl.BlockSpec((tm, d), lambda i: (i, 0)),
                  pl.BlockSpec((1, d), lambda i: (0, 0)),
                  pl.BlockSpec((d, width), lambda i: (0, 0)),
                  pl.BlockSpec((mem_tokens, 2 * width), lambda i: (i // per_seq, layer)),
                  pl.BlockSpec((width, d), lambda i: (0, 0))],
        out_specs=pl.BlockSpec((tm, d), lambda i: (i, 0)),
        out_shape=jax.ShapeDtypeStruct((m, d), F32),
        compiler_params=_cparams("parallel"),
        name="cross_attn",
    )(x, g.reshape(1, d), wq, kv, wo)


def _rmsnorm_kernel(x_ref, g_ref, o_ref):
    o_ref[...] = _rms(x_ref[...], g_ref[...])


def _rmsnorm(x, g, *, tm):
    m, d = x.shape
    tm = min(tm, m)
    return pl.pallas_call(
        _rmsnorm_kernel,
        grid=(m // tm,),
        in_specs=[pl.BlockSpec((tm, d), lambda i: (i, 0)), pl.BlockSpec((1, d), lambda i: (0, 0))],
        out_specs=pl.BlockSpec((tm, d), lambda i: (i, 0)),
        out_shape=jax.ShapeDtypeStruct((m, d), F32),
        compiler_params=_cparams("parallel"),
        name="final_rmsnorm",
    )(x, g.reshape(1, d))


TM = 512
TN = 1024
TF = 512


def _moba_deltanet_mixer(x, g, w_in, conv_w, a_log, dt_bias, o_norm_g, w_out, *, batch):
    m, d = x.shape
    seq = m // batch
    dn_heads = a_log.shape[0]
    main = w_in.shape[1] - 2 * dn_heads
    dn_width = dn_heads * HEAD_DIM
    moba_heads = (main - 4 * dn_width) // (3 * HEAD_DIM)
    proj = _norm_matmul(x, g, w_in[:, :main].astype(MXU_DTYPE), tm=TM, tn=TN)
    w_small = jnp.pad(w_in[:, main:], ((0, 0), (0, HEAD_DIM - 2 * dn_heads))).astype(MXU_DTYPE)
    small = _norm_matmul(x, g, w_small, tm=TM, tn=HEAD_DIM)
    proj = proj.reshape(batch, seq, main)
    small = small.reshape(batch, seq, HEAD_DIM)
    y_a = _moba(proj, heads=moba_heads)
    y_b = _deltanet(proj, small, conv_w, a_log, dt_bias, o_norm_g, col0=3 * moba_heads, heads=dn_heads)
    wa = w_out[:moba_heads * HEAD_DIM].astype(MXU_DTYPE)
    wb = w_out[moba_heads * HEAD_DIM:].astype(MXU_DTYPE)
    return _matmul_residual(x, [y_a.reshape(m, -1), y_b.reshape(m, -1)], [wa, wb], tm=TM, tn=TN)


def _hgrn2_mixer(x, g, w_in, lb_logits, o_norm_g, w_out, *, batch, layer):
    m, d = x.shape
    seq = m // batch
    width = w_in.shape[1] // 4
    proj = _norm_matmul(x, g, w_in.astype(MXU_DTYPE), tm=TM, tn=TN).reshape(batch, seq, 4 * width)
    y = _hgrn2(proj, lb_logits, o_norm_g, layer=layer, heads=width // HEAD_DIM)
    return _matmul_residual(x, [y.reshape(m, width)], [w_out.astype(MXU_DTYPE)], tm=TM, tn=TN)


def kernel(x, mem, norm_g, mem_norm_g, final_norm_g, ffn_w_in, ffn_w_out, ab_w_in, ab_conv_w, ab_a_log, ab_dt_bias, ab_o_norm_g, ab_w_out, c_w_in, c_lb_logits, c_o_norm_g, c_w_out, x_w_q, x_w_kv, x_w_o):
    batch, seq, d = x.shape
    depth = norm_g.shape[0]
    m = batch * seq
    xf = x.reshape(m, d).astype(F32)
    w_kv = jnp.transpose(x_w_kv, (1, 0, 2)).reshape(d, -1).astype(MXU_DTYPE)
    kv = _norm_matmul(mem.reshape(-1, d).astype(F32), mem_norm_g, w_kv, tm=TM, tn=TN)
    for l in range(depth):
        xf = _ffn(xf, norm_g[l, 0], *_ffn_weights(ffn_w_in[l, 0], ffn_w_out[l, 0], TF), tm=TM, tf=TF)
        if l % 2 == 0:
            e = l // 2
            xf = _moba_deltanet_mixer(xf, norm_g[l, 1], ab_w_in[e], ab_conv_w[e], ab_a_log[e], ab_dt_bias[e],
                                      ab_o_norm_g[e], ab_w_out[e], batch=batch)
        else:
            o = l // 2
            xf = _hgrn2_mixer(xf, norm_g[l, 1], c_w_in[o], c_lb_logits, c_o_norm_g[o], c_w_out[o],
                              batch=batch, layer=l)
        xf = _cross_attn(xf, norm_g[l, 2], x_w_q[l].astype(MXU_DTYPE), kv, x_w_o[l].astype(MXU_DTYPE),
                         layer=l, seq=seq, tm=TM)
        xf = _ffn(xf, norm_g[l, 3], *_ffn_weights(ffn_w_in[l, 1], ffn_w_out[l, 1], TF), tm=TM, tf=TF)
    return _rmsnorm(xf, final_norm_g, tm=TM).reshape(batch, seq, d)
```

```python
import functools

import jax
import jax.numpy as jnp
from jax import lax
from jax.experimental import pallas as pl
from jax.experimental.pallas import tpu as pltpu

F32 = jnp.float32
MXU_DTYPE = jnp.bfloat16
HIGHEST = lax.Precision.HIGHEST
EPS = 1e-6
NEG_INF = float("-inf")

HEAD_DIM = 128
MOBA_BLOCK = 256
MOBA_TOPK = 3
MOBA_STEP = 2
CHUNK = 64
SUB = 16
CONV_W = 4
CONV_HALO = 8
X_HEADS = 4
HEAD_GROUP = 4
SEQ_TILE = 512
VMEM_LIMIT = 56 * 1024 * 1024


def _cparams(*sem):
    return pltpu.CompilerParams(dimension_semantics=sem, vmem_limit_bytes=VMEM_LIMIT)


def _tile(n, t):
    if n <= t:
        return n
    t -= t % HEAD_DIM
    while n % t:
        t -= HEAD_DIM
    return t


def _mm(a, b):
    return jnp.dot(a.astype(MXU_DTYPE), b.astype(MXU_DTYPE), preferred_element_type=F32)


def _mm_nt(a, b):
    return lax.dot_general(a.astype(MXU_DTYPE), b.astype(MXU_DTYPE), (((1,), (1,)), ((), ())),
                           preferred_element_type=F32)


def _mm_tn(a, b):
    return lax.dot_general(a.astype(MXU_DTYPE), b.astype(MXU_DTYPE), (((0,), (0,)), ((), ())),
                           preferred_element_type=F32)


def _split2(x):
    hi = x.astype(MXU_DTYPE)
    return hi, (x - hi.astype(F32)).astype(MXU_DTYPE)


def _mm_x3(a, b):
    ah, al = _split2(a)
    bh, bl = _split2(b)
    dot = functools.partial(jnp.dot, preferred_element_type=F32)
    return dot(ah, bh) + (dot(ah, bl) + dot(al, bh))


def _cumsum_rows(ones_tril, x):
    hi = x.astype(MXU_DTYPE)
    rest = x - hi.astype(F32)
    mid = rest.astype(MXU_DTYPE)
    lo = (rest - mid.astype(F32)).astype(MXU_DTYPE)
    dot = functools.partial(jnp.dot, ones_tril.astype(MXU_DTYPE), preferred_element_type=F32)
    return dot(hi) + (dot(mid) + dot(lo))


def _sigmoid(x):
    return 1.0 / (1.0 + jnp.exp(-x))


def _silu(x):
    return x * _sigmoid(x)


def _rms(x, g):
    return x * lax.rsqrt(jnp.mean(x * x, axis=-1, keepdims=True) + EPS) * g


def _norm_matmul_kernel(x_ref, g_ref, w_ref, o_ref, xn_ref):
    @pl.when(pl.program_id(1) == 0)
    def _():
        xn_ref[...] = _rms(x_ref[...], g_ref[...]).astype(xn_ref.dtype)

    o_ref[...] = jnp.dot(xn_ref[...], w_ref[...], preferred_element_type=F32)


def _norm_matmul(x, g, w, *, tm, tn):
    m, k = x.shape
    n = w.shape[1]
    tm, tn = _tile(m, tm), _tile(n, tn)
    return pl.pallas_call(
        _norm_matmul_kernel,
        grid=(m // tm, n // tn),
        in_specs=[pl.BlockSpec((tm, k), lambda i, j: (i, 0)),
                  pl.BlockSpec((1, k), lambda i, j: (0, 0)),
                  pl.BlockSpec((k, tn), lambda i, j: (0, j))],
        out_specs=pl.BlockSpec((tm, tn), lambda i, j: (i, j)),
        out_shape=jax.ShapeDtypeStruct((m, n), F32),
        scratch_shapes=[pltpu.VMEM((tm, k), MXU_DTYPE)],
        compiler_params=_cparams("parallel", "arbitrary"),
        name="norm_matmul",
    )(x, g.reshape(1, k), w)


def _ffn_kernel(x_ref, g_ref, wa_ref, wb_ref, wo_ref, o_ref, xn_ref, acc_ref):
    j = pl.program_id(1)

    @pl.when(j == 0)
    def _():
        xn_ref[...] = _rms(x_ref[...], g_ref[...]).astype(xn_ref.dtype)
        acc_ref[...] = jnp.zeros_like(acc_ref)

    xn = xn_ref[...]
    a = jnp.dot(xn, wa_ref[...], preferred_element_type=F32)
    b = jnp.dot(xn, wb_ref[...], preferred_element_type=F32)
    acc_ref[...] += _mm(_silu(a) * b, wo_ref[...])

    @pl.when(j == pl.num_programs(1) - 1)
    def _():
        o_ref[...] = x_ref[...] + 0.5 * acc_ref[...]


def _ffn(x, g, wa, wb, wo, *, tm, tf):
    m, d = x.shape
    f = wa.shape[1]
    tm = min(tm, m)
    assert m % tm == 0 and f % tf == 0
    return pl.pallas_call(
        _ffn_kernel,
        grid=(m // tm, f // tf),
        in_specs=[pl.BlockSpec((tm, d), lambda i, j: (i, 0)),
                  pl.BlockSpec((1, d), lambda i, j: (0, 0)),
                  pl.BlockSpec((d, tf), lambda i, j: (0, j)),
                  pl.BlockSpec((d, tf), lambda i, j: (0, j)),
                  pl.BlockSpec((tf, d), lambda i, j: (j, 0))],
        out_specs=pl.BlockSpec((tm, d), lambda i, j: (i, 0)),
        out_shape=jax.ShapeDtypeStruct((m, d), F32),
        scratch_shapes=[pltpu.VMEM((tm, d), MXU_DTYPE), pltpu.VMEM((tm, d), F32)],
        compiler_params=_cparams("parallel", "arbitrary"),
        name="ffn",
    )(x, g.reshape(1, d), wa, wb, wo)


def _ffn_weights(w_in, w_out, tf):
    f = w_out.shape[0]
    pad = (-f) % tf
    wa = jnp.pad(w_in[:, :f], ((0, 0), (0, pad))).astype(MXU_DTYPE)
    wb = jnp.pad(w_in[:, f:], ((0, 0), (0, pad))).astype(MXU_DTYPE)
    wo = jnp.pad(w_out, ((0, pad), (0, 0))).astype(MXU_DTYPE)
    return wa, wb, wo


def _matmul_residual_kernel(*refs, n_in):
    x_ref, o_ref = refs[0], refs[-1]
    acc = x_ref[...]
    for i in range(n_in):
        acc = acc + _mm(refs[1 + i][...], refs[1 + n_in + i][...])
    o_ref[...] = acc


def _matmul_residual(x, ys, ws, *, tm, tn):
    m, n = x.shape
    tm, tn = _tile(m, tm), _tile(n, tn)
    n_in = len(ys)
    in_specs = [pl.BlockSpec((tm, tn), lambda i, j: (i, j))]
    in_specs += [pl.BlockSpec((tm, y.shape[1]), lambda i, j: (i, 0)) for y in ys]
    in_specs += [pl.BlockSpec((w.shape[0], tn), lambda i, j: (0, j)) for w in ws]
    return pl.pallas_call(
        functools.partial(_matmul_residual_kernel, n_in=n_in),
        grid=(m // tm, n // tn),
        in_specs=in_specs,
        out_specs=pl.BlockSpec((tm, tn), lambda i, j: (i, j)),
        out_shape=jax.ShapeDtypeStruct((m, n), F32),
        compiler_params=_cparams("parallel", "arbitrary"),
        name="matmul_residual",
    )(x, *ys, *ws)


def _moba_kernel(q_ref, k_ref, v_ref, o_ref, kb_ref, vt_ref, kmean_ref, sel_ref, *, seq):
    qi = pl.program_id(2)
    nb = seq // MOBA_BLOCK
    rows = MOBA_BLOCK

    @pl.when(qi == 0)
    def _():
        for n in range(nb):
            kn = k_ref[n * rows:(n + 1) * rows, :]
            kb_ref[n] = kn.astype(kb_ref.dtype)
            kmean_ref[n:n + 1, :] = jnp.mean(kn, axis=0, keepdims=True)
            vt_ref[n] = v_ref[n * rows:(n + 1) * rows, :].T.astype(vt_ref.dtype)

    q = q_ref[...]
    gate = lax.dot_general(kmean_ref[...], q, (((1,), (1,)), ((), ())), precision=HIGHEST,
                           preferred_element_type=F32)
    blk = lax.broadcasted_iota(jnp.int32, (nb, rows), 0)
    past = blk < qi
    gate = jnp.where(past, gate, NEG_INF)
    rank = jnp.zeros((nb, rows), jnp.int32)
    for m in range(nb):
        gm = gate[m:m + 1, :]
        beats = (gm > gate) | ((gm == gate) & (m < blk))
        rank = rank + beats.astype(jnp.int32)
    sel_ref[...] = jnp.where(past & (rank < MOBA_TOPK), 1.0, 0.0)

    scale = HEAD_DIM ** -0.5
    qb = q.astype(kb_ref.dtype)
    key = lax.broadcasted_iota(jnp.int32, (rows, rows), 0)
    qry = lax.broadcasted_iota(jnp.int32, (rows, rows), 1)
    s = _mm_nt(kb_ref[qi], qb) * scale
    s = jnp.where(key <= qry, s, NEG_INF)
    m0 = jnp.max(s, axis=0, keepdims=True)
    p = jnp.exp(s - m0)
    l0 = jnp.sum(p, axis=0, keepdims=True)
    acc0 = _mm(vt_ref[qi], p)

    def body(i, carry):
        m_run, l_run, acc = carry
        n0 = MOBA_STEP * i
        ss = []
        for d in range(MOBA_STEP):
            s = _mm_nt(kb_ref[n0 + d], qb) * scale
            ss.append(jnp.where(sel_ref[pl.ds(n0 + d, 1), :] > 0.0, s, NEG_INF))
        m_new = m_run
        for s in ss:
            m_new = jnp.maximum(m_new, jnp.max(s, axis=0, keepdims=True))
        alpha = jnp.exp(m_run - m_new)
        ps = [jnp.exp(s - m_new) for s in ss]
        l_new = alpha * l_run
        acc = alpha * acc
        for d, p in enumerate(ps):
            l_new = l_new + jnp.sum(p, axis=0, keepdims=True)
            acc = acc + _mm(vt_ref[n0 + d], p)
        return m_new, l_new, acc

    trips = (qi + MOBA_STEP - 1) // MOBA_STEP
    _, l_fin, acc = lax.fori_loop(0, trips, body, (m0, l0, acc0))
    o_ref[...] = (acc / l_fin).T


def _moba(proj, *, heads):
    b, t, _ = proj.shape
    nb = t // MOBA_BLOCK
    assert t % MOBA_BLOCK == 0 and nb % MOBA_STEP == 0
    return pl.pallas_call(
        functools.partial(_moba_kernel, seq=t),
        grid=(b, heads, nb),
        in_specs=[pl.BlockSpec((None, MOBA_BLOCK, HEAD_DIM), lambda bi, h, i: (bi, i, h)),
                  pl.BlockSpec((None, t, HEAD_DIM), lambda bi, h, i: (bi, 0, heads + h)),
                  pl.BlockSpec((None, t, HEAD_DIM), lambda bi, h, i: (bi, 0, 2 * heads + h))],
        out_specs=pl.BlockSpec((None, MOBA_BLOCK, HEAD_DIM), lambda bi, h, i: (bi, i, h)),
        out_shape=jax.ShapeDtypeStruct((b, t, heads * HEAD_DIM), F32),
        scratch_shapes=[pltpu.VMEM((nb, MOBA_BLOCK, HEAD_DIM), MXU_DTYPE),
                        pltpu.VMEM((nb, HEAD_DIM, MOBA_BLOCK), MXU_DTYPE),
                        pltpu.VMEM((nb, HEAD_DIM), F32),
                        pltpu.VMEM((nb, MOBA_BLOCK), F32)],
        compiler_params=_cparams("parallel", "parallel", "arbitrary"),
        name="moba",
    )(proj, proj, proj)


def _chunk_masks():
    row = lax.broadcasted_iota(jnp.int32, (CHUNK, CHUNK), 0)
    col = lax.broadcasted_iota(jnp.int32, (CHUNK, CHUNK), 1)
    return row >= col, row > col, (row == col).astype(F32)


def _conv_silu(prev, x, cw):
    xe = jnp.concatenate([prev, x], axis=0)
    acc = x * cw[CONV_W - 1:CONV_W, :]
    for k in range(1, CONV_W):
        acc = acc + pltpu.roll(xe, k, axis=0)[CONV_HALO:, :] * cw[CONV_W - 1 - k:CONV_W - k, :]
    return _silu(acc)


def _unit_lower_inverse(m_mats, eye):
    pows = [[-m] for m in m_mats]
    for _ in range(CHUNK.bit_length() - 2):
        for p in pows:
            p.append(_mm_x3(p[-1], p[-1]))
    terms = [[eye + pw for pw in p] for p in pows]
    while len(terms[0]) > 1:
        terms = [[_mm_x3(t[i], t[i + 1]) for i in range(0, len(t) - 1, 2)] + t[len(t) - len(t) % 2:]
                 for t in terms]
    return [t[0] for t in terms]


def _deltanet_kernel(q_ref, k_ref, v_ref, gate_ref, small_ref, cwq_ref, cwk_ref, cwv_ref,
                     alog_ref, dt_ref, ong_ref, o_ref, state_ref, halo_ref, *, heads, group, tile):
    hg = pl.program_id(1)
    tril, strict, eye = _chunk_masks()
    ones_tril = tril.astype(F32)
    lane = lax.broadcasted_iota(jnp.int32, (CHUNK, HEAD_DIM), 1)
    ong = ong_ref[...]

    @pl.when(pl.program_id(2) == 0)
    def _():
        state_ref[...] = jnp.zeros_like(state_ref)
        halo_ref[...] = jnp.zeros_like(halo_ref)

    grp = range(group)
    cols = [slice(j * HEAD_DIM, (j + 1) * HEAD_DIM) for j in grp]

    def body(c, carry):
        rows = pl.ds(pl.multiple_of(c * CHUNK, CHUNK), CHUNK)
        sm = small_ref[rows, :]
        dq, dk, dv, beta, g = [], [], [], [], []
        for j in grp:
            head = hg * group + j
            xq, xk, xv = q_ref[rows, cols[j]], k_ref[rows, cols[j]], v_ref[rows, cols[j]]
            cq = _conv_silu(halo_ref[3 * j], xq, cwq_ref[:, cols[j]])
            ck = _conv_silu(halo_ref[3 * j + 1], xk, cwk_ref[:, cols[j]])
            dv.append(_conv_silu(halo_ref[3 * j + 2], xv, cwv_ref[:, cols[j]]))
            halo_ref[3 * j] = xq[CHUNK - CONV_HALO:, :]
            halo_ref[3 * j + 1] = xk[CHUNK - CONV_HALO:, :]
            halo_ref[3 * j + 2] = xv[CHUNK - CONV_HALO:, :]
            dq.append(cq * lax.rsqrt(jnp.sum(cq * cq, axis=-1, keepdims=True) + EPS) * (HEAD_DIM ** -0.5))
            dk.append(ck * lax.rsqrt(jnp.sum(ck * ck, axis=-1, keepdims=True) + EPS))
            beta_raw = jnp.sum(jnp.where(lane == head, sm, 0.0), axis=1, keepdims=True)
            alpha_raw = jnp.sum(jnp.where(lane == heads + head, sm, 0.0), axis=1, keepdims=True)
            beta.append(_sigmoid(beta_raw))
            z = alpha_raw + dt_ref[pl.ds(head, 1), :]
            g.append(-jnp.exp(alog_ref[pl.ds(head, 1), :])
                     * (jnp.maximum(z, 0.0) + jnp.log1p(jnp.exp(-jnp.abs(z)))))
        gc = [_cumsum_rows(ones_tril, g[j]) for j in grp]
        gamma, e_g, kb = [], [], []
        for j in grp:
            g_col = gc[j][:, :CHUNK]
            g_row = jnp.sum(g_col * eye, axis=0, keepdims=True)
            gamma.append(jnp.exp(jnp.where(tril, g_col - g_row, NEG_INF)))
            e_g.append(jnp.exp(gc[j]))
            kb.append(dk[j] * beta[j])
        kq = [_mm_nt(jnp.concatenate([kb[j], dq[j]], axis=0), dk[j]) for j in grp]
        t_inv = _unit_lower_inverse([jnp.where(strict, kq[j][:CHUNK] * gamma[j], 0.0) for j in grp], eye)
        a_qk = [kq[j][CHUNK:] * gamma[j] for j in grp]
        uw = [_mm(t_inv[j], jnp.concatenate([dv[j] * beta[j], kb[j] * e_g[j]], axis=1)) for j in grp]
        state = [state_ref[j] for j in grp]
        ws = [_mm(jnp.concatenate([uw[j][:, HEAD_DIM:], dq[j] * e_g[j]], axis=0), state[j]) for j in grp]
        v_new = [uw[j][:, :HEAD_DIM] - ws[j][:CHUNK] for j in grp]
        o = [ws[j][CHUNK:] + _mm(a_qk[j], v_new[j]) for j in grp]
        for j in grp:
            g_last = gc[j][CHUNK - 1:CHUNK, :]
            state_ref[j] = jnp.exp(g_last) * state[j] + _mm_tn(dk[j] * jnp.exp(g_last - gc[j]), v_new[j])
        for j in grp:
            o_ref[rows, cols[j]] = _rms(o[j], ong) * _silu(gate_ref[rows, cols[j]])
        return carry

    lax.fori_loop(0, tile // CHUNK, body, 0)


def _deltanet(proj, small, conv_w, a_log, dt_bias, o_norm_g, *, col0, heads):
    b, t, _ = proj.shape
    group = min(HEAD_GROUP, heads)
    tile = min(SEQ_TILE, t)
    assert t % tile == 0 and tile % CHUNK == 0 and heads % group == 0 and col0 % group == 0
    width = group * HEAD_DIM

    def col(off):
        base = (col0 + off * heads) // group
        return pl.BlockSpec((None, tile, width), lambda bi, hg, ti: (bi, ti, base + hg))

    def cw(off):
        base = (off * heads) // group
        return pl.BlockSpec((CONV_W, width), lambda bi, hg, ti: (0, base + hg))

    rep = lambda a: jnp.broadcast_to(a.astype(F32)[:, None], (heads, HEAD_DIM))
    whole = lambda r: pl.BlockSpec((r, HEAD_DIM), lambda bi, hg, ti: (0, 0))
    return pl.pallas_call(
        functools.partial(_deltanet_kernel, heads=heads, group=group, tile=tile),
        grid=(b, heads // group, t // tile),
        in_specs=[col(0), col(1), col(2), col(3),
                  pl.BlockSpec((None, tile, HEAD_DIM), lambda bi, hg, ti: (bi, ti, 0)),
                  cw(0), cw(1), cw(2), whole(heads), whole(heads), whole(1)],
        out_specs=pl.BlockSpec((None, tile, width), lambda bi, hg, ti: (bi, ti, hg)),
        out_shape=jax.ShapeDtypeStruct((b, t, heads * HEAD_DIM), F32),
        scratch_shapes=[pltpu.VMEM((group, HEAD_DIM, HEAD_DIM), F32),
                        pltpu.VMEM((3 * group, CONV_HALO, HEAD_DIM), F32)],
        compiler_params=_cparams("parallel", "parallel", "arbitrary"),
        name="deltanet",
    )(proj, proj, proj, proj, small, conv_w, conv_w, conv_w, rep(a_log), rep(dt_bias),
      o_norm_g.reshape(1, HEAD_DIM))


def _hgrn2_kernel(q_ref, f_ref, i_ref, g_ref, lbl_ref, ong_ref, o_ref, state_ref, *, layer, group, tile):
    tril, _, _ = _chunk_masks()
    ones_tril = tril.astype(F32)
    lbl = lbl_ref[...]
    e = jnp.exp(lbl - jnp.max(lbl, axis=0, keepdims=True))
    p = e / jnp.sum(e, axis=0, keepdims=True)
    cs = p[0:1, :]
    for r in range(1, layer + 1):
        cs = cs + p[r:r + 1, :]
    lb_all = cs - p[0:1, :]
    ong = ong_ref[...]
    pos = lax.broadcasted_iota(jnp.int32, (CHUNK, 1), 0) % SUB

    @pl.when(pl.program_id(2) == 0)
    def _():
        state_ref[...] = jnp.zeros_like(state_ref)

    grp = range(group)
    cols = [slice(j * HEAD_DIM, (j + 1) * HEAD_DIM) for j in grp]

    def body(c, carry):
        rows = pl.ds(pl.multiple_of(c * CHUNK, CHUNK), CHUNK)
        q, k, v, log_f = [], [], [], []
        for j in grp:
            lb = lb_all[:, cols[j]]
            fr = f_ref[rows, cols[j]]
            log_f.append(jnp.log(lb + (1.0 - lb) * _sigmoid(fr)))
            k.append((1.0 - lb) * _sigmoid(-fr))
            q.append(_silu(q_ref[rows, cols[j]]) * (HEAD_DIM ** -0.5))
            v.append(i_ref[rows, cols[j]])
        b = [_cumsum_rows(ones_tril, log_f[j]) for j in grp]
        state = [state_ref[j] for j in grp]
        o = [_mm_nt(q[j] * jnp.exp(b[j]), state[j]) for j in grp]
        scores = []
        for i in range(1, CHUNK // SUB):
            lo = i * SUB
            for j in grp:
                ref_row = b[j][lo:lo + 1, :]
                qt = q[j][lo:lo + SUB, :] * jnp.exp(b[j][lo:lo + SUB, :] - ref_row)
                kt = k[j][:lo, :] * jnp.exp(ref_row - b[j][:lo, :])
                scores.append(_mm_nt(qt, kt))
        below = [[jnp.zeros((SUB, HEAD_DIM), F32)] for _ in grp]
        for i in range(1, CHUNK // SUB):
            for j in grp:
                below[j].append(_mm(scores[(i - 1) * group + j], v[j][:i * SUB, :]))
        o = [o[j] + jnp.concatenate(below[j], axis=0) for j in grp]
        for d in range(SUB):
            for j in grp:
                kr, br, vr = ((k[j], b[j], v[j]) if d == 0 else
                              tuple(pltpu.roll(a, d, axis=0) for a in (k[j], b[j], v[j])))
                decay = jnp.exp(jnp.where(pos >= d, b[j] - br, NEG_INF))
                o[j] = o[j] + jnp.sum(q[j] * kr * decay, axis=1, keepdims=True) * vr
        for j in grp:
            b_last = b[j][CHUNK - 1:CHUNK, :]
            state_ref[j] = state[j] * jnp.exp(b_last) + _mm_tn(v[j], k[j] * jnp.exp(b_last - b[j]))
        for j in grp:
            o_ref[rows, cols[j]] = _rms(o[j], ong) * _silu(g_ref[rows, cols[j]])
        return carry

    lax.fori_loop(0, tile // CHUNK, body, 0)


def _hgrn2(proj, lb_logits, o_norm_g, *, layer, heads):
    b, t, _ = proj.shape
    depth = lb_logits.shape[0]
    group = min(HEAD_GROUP, heads)
    tile = min(SEQ_TILE, t)
    assert t % tile == 0 and tile % CHUNK == 0 and heads % group == 0
    width = group * HEAD_DIM

    def col(off):
        base = (off * heads) // group
        return pl.BlockSpec((None, tile, width), lambda bi, hg, ti: (bi, ti, base + hg))

    return pl.pallas_call(
        functools.partial(_hgrn2_kernel, layer=layer, group=group, tile=tile),
        grid=(b, heads // group, t // tile),
        in_specs=[col(0), col(1), col(2), col(3),
                  pl.BlockSpec((depth, width), lambda bi, hg, ti: (0, hg)),
                  pl.BlockSpec((1, HEAD_DIM), lambda bi, hg, ti: (0, 0))],
        out_specs=pl.BlockSpec((None, tile, width), lambda bi, hg, ti: (bi, ti, hg)),
        out_shape=jax.ShapeDtypeStruct((b, t, heads * HEAD_DIM), F32),
        scratch_shapes=[pltpu.VMEM((group, HEAD_DIM, HEAD_DIM), F32)],
        compiler_params=_cparams("parallel", "parallel", "arbitrary"),
        name="hgrn2",
    )(proj, proj, proj, proj, lb_logits.astype(F32), o_norm_g.reshape(1, HEAD_DIM))


def _cross_attn_kernel(x_ref, g_ref, wq_ref, kv_ref, wo_ref, o_ref):
    x = x_ref[...]
    xn = _rms(x, g_ref[...])
    q = _mm(xn, wq_ref[...])
    width = X_HEADS * HEAD_DIM
    scale = HEAD_DIM ** -0.5
    outs = []
    for h in range(X_HEADS):
        lo = h * HEAD_DIM
        s = _mm_nt(q[:, lo:lo + HEAD_DIM], kv_ref[:, lo:lo + HEAD_DIM]) * scale
        p = jnp.exp(s - jnp.max(s, axis=-1, keepdims=True))
        p = p / jnp.sum(p, axis=-1, keepdims=True)
        outs.append(_mm(p, kv_ref[:, width + lo:width + lo + HEAD_DIM]))
    o_ref[...] = x + _mm(jnp.concatenate(outs, axis=-1), wo_ref[...])


def _cross_attn(x, g, wq, kv, wo, *, layer, seq, tm):
    m, d = x.shape
    mem_tokens = kv.shape[0] // (m // seq)
    tm = min(tm, seq)
    assert seq % tm == 0
    per_seq = seq // tm
    width = X_HEADS * HEAD_DIM
    return pl.pallas_call(
        _cross_attn_kernel,
        grid=(m // tm,),
        in_specs=[pl.BlockSpec((tm, d), lambda i: (i, 0)),
                  pl.BlockSpec((1, d), lambda i: (0, 0)),
                  pl.BlockSpec((d, width), lambda i: (0, 0)),
                  pl.BlockSpec((mem_tokens, 2 * width), lambda i: (i // per_seq, layer)),
                  pl.BlockSpec((width, d), lambda i: (0, 0))],
        out_specs=pl.BlockSpec((tm, d), lambda i: (i, 0)),
        out_shape=jax.ShapeDtypeStruct((m, d), F32),
        compiler_params=_cparams("parallel"),
        name="cross_attn",
    )(x, g.reshape(1, d), wq, kv, wo)


def _rmsnorm_kernel(x_ref, g_ref, o_ref):
    o_ref[...] = _rms(x_ref[...], g_ref[...])


def _rmsnorm(x, g, *, tm):
    m, d = x.shape
    tm = min(tm, m)
    return pl.pallas_call(
        _rmsnorm_kernel,
        grid=(m // tm,),
        in_specs=[pl.BlockSpec((tm, d), lambda i: (i, 0)), pl.BlockSpec((1, d), lambda i: (0, 0))],
        out_specs=pl.BlockSpec((tm, d), lambda i: (i, 0)),
        out_shape=jax.ShapeDtypeStruct((m, d), F32),
        compiler_params=_cparams("parallel"),
        name="final_rmsnorm",
    )(x, g.reshape(1, d))


TM = 512
TN = 1024
TF = 512


def _moba_deltanet_mixer(x, g, w_in, conv_w, a_log, dt_bias, o_norm_g, w_out, *, batch):
    m, d = x.shape
    seq = m // batch
    dn_heads = a_log.shape[0]
    main = w_in.shape[1] - 2 * dn_heads
    dn_width = dn_heads * HEAD_DIM
    moba_heads = (main - 4 * dn_width) // (3 * HEAD_DIM)
    proj = _norm_matmul(x, g, w_in[:, :main].astype(MXU_DTYPE), tm=TM, tn=TN)
    w_small = jnp.pad(w_in[:, main:], ((0, 0), (0, HEAD_DIM - 2 * dn_heads))).astype(MXU_DTYPE)
    small = _norm_matmul(x, g, w_small, tm=TM, tn=HEAD_DIM)
    proj = proj.reshape(batch, seq, main)
    small = small.reshape(batch, seq, HEAD_DIM)
    y_a = _moba(proj, heads=moba_heads)
    y_b = _deltanet(proj, small, conv_w, a_log, dt_bias, o_norm_g, col0=3 * moba_heads, heads=dn_heads)
    wa = w_out[:moba_heads * HEAD_DIM].astype(MXU_DTYPE)
    wb = w_out[moba_heads * HEAD_DIM:].astype(MXU_DTYPE)
    return _matmul_residual(x, [y_a.reshape(m, -1), y_b.reshape(m, -1)], [wa, wb], tm=TM, tn=TN)


def _hgrn2_mixer(x, g, w_in, lb_logits, o_norm_g, w_out, *, batch, layer):
    m, d = x.shape
    seq = m // batch
    width = w_in.shape[1] // 4
    proj = _norm_matmul(x, g, w_in.astype(MXU_DTYPE), tm=TM, tn=TN).reshape(batch, seq, 4 * width)
    y = _hgrn2(proj, lb_logits, o_norm_g, layer=layer, heads=width // HEAD_DIM)
    return _matmul_residual(x, [y.reshape(m, width)], [w_out.astype(MXU_DTYPE)], tm=TM, tn=TN)


def kernel(x, mem, norm_g, mem_norm_g, final_norm_g, ffn_w_in, ffn_w_out, ab_w_in, ab_conv_w, ab_a_log, ab_dt_bias, ab_o_norm_g, ab_w_out, c_w_in, c_lb_logits, c_o_norm_g, c_w_out, x_w_q, x_w_kv, x_w_o):
    batch, seq, d = x.shape
    depth = norm_g.shape[0]
    m = batch * seq
    xf = x.reshape(m, d).astype(F32)
    w_kv = jnp.transpose(x_w_kv, (1, 0, 2)).reshape(d, -1).astype(MXU_DTYPE)
    kv = _norm_matmul(mem.reshape(-1, d).astype(F32), mem_norm_g, w_kv, tm=TM, tn=TN)
    for l in range(depth):
        xf = _ffn(xf, norm_g[l, 0], *_ffn_weights(ffn_w_in[l, 0], ffn_w_out[l, 0], TF), tm=TM, tf=TF)
        if l % 2 == 0:
            e = l // 2
            xf = _moba_deltanet_mixer(xf, norm_g[l, 1], ab_w_in[e], ab_conv_w[e], ab_a_log[e], ab_dt_bias[e],
                                      ab_o_norm_g[e], ab_w_out[e], batch=batch)
        else:
            o = l // 2
            xf = _hgrn2_mixer(xf, norm_g[l, 1], c_w_in[o], c_lb_logits, c_o_norm_g[o], c_w_out[o],
                              batch=batch, layer=l)
        xf = _cross_attn(xf, norm_g[l, 2], x_w_q[l].astype(MXU_DTYPE), kv, x_w_o[l].astype(MXU_DTYPE),
                         layer=l, seq=seq, tm=TM)
        xf = _ffn(xf, norm_g[l, 3], *_ffn_weights(ffn_w_in[l, 1], ffn_w_out[l, 1], TF), tm=TM, tf=TF)
    return _rmsnorm(xf, final_norm_g, tm=TM).reshape(batch, seq, d)
```

```python
import functools

import jax
import jax.numpy as jnp
from jax import lax
from jax.experimental import pallas as pl
from jax.experimental.pallas import tpu as pltpu

F32 = jnp.float32
MXU_DTYPE = jnp.bfloat16
HIGHEST = lax.Precision.HIGHEST
EPS = 1e-6
NEG_INF = float("-inf")

HEAD_DIM = 128
MOBA_BLOCK = 256
MOBA_TOPK = 3
MOBA_STEP = 2
CHUNK = 64
SUB = 16
HALF = SUB // 2
CONV_W = 4
CONV_HALO = 8
X_HEADS = 4
MOBA_GROUP = 2
DN_GROUP = 8
HEAD_GROUP = 4
SEQ_TILE = 512
VMEM_LIMIT = 56 * 1024 * 1024


def _cparams(*sem):
    return pltpu.CompilerParams(dimension_semantics=sem, vmem_limit_bytes=VMEM_LIMIT)


def _tile(n, t):
    if n <= t:
        return n
    t -= t % HEAD_DIM
    while n % t:
        t -= HEAD_DIM
    return t


def _mm(a, b):
    return jnp.dot(a.astype(MXU_DTYPE), b.astype(MXU_DTYPE), preferred_element_type=F32)


def _mm_nt(a, b):
    return lax.dot_general(a.astype(MXU_DTYPE), b.astype(MXU_DTYPE), (((1,), (1,)), ((), ())),
                           preferred_element_type=F32)


def _mm_tn(a, b):
    return lax.dot_general(a.astype(MXU_DTYPE), b.astype(MXU_DTYPE), (((0,), (0,)), ((), ())),
                           preferred_element_type=F32)


def _split2(x):
    hi = x.astype(MXU_DTYPE)
    return hi, (x - hi.astype(F32)).astype(MXU_DTYPE)


def _mm_x3(a, b):
    ah, al = _split2(a)
    bh, bl = _split2(b)
    dot = functools.partial(jnp.dot, preferred_element_type=F32)
    return dot(ah, bh) + (dot(ah, bl) + dot(al, bh))


def _cumsum_rows(ones_tril, x):
    hi = x.astype(MXU_DTYPE)
    rest = x - hi.astype(F32)
    mid = rest.astype(MXU_DTYPE)
    lo = (rest - mid.astype(F32)).astype(MXU_DTYPE)
    dot = functools.partial(jnp.dot, ones_tril.astype(MXU_DTYPE), preferred_element_type=F32)
    return dot(hi) + (dot(mid) + dot(lo))


def _sigmoid(x):
    return 1.0 / (1.0 + jnp.exp(-x))


def _silu(x):
    return x * _sigmoid(x)


def _rms(x, g):
    return x * lax.rsqrt(jnp.mean(x * x, axis=-1, keepdims=True) + EPS) * g


def _norm_matmul_kernel(x_ref, g_ref, w_ref, o_ref, xn_ref):
    @pl.when(pl.program_id(1) == 0)
    def _():
        xn_ref[...] = _rms(x_ref[...], g_ref[...]).astype(xn_ref.dtype)

    o_ref[...] = jnp.dot(xn_ref[...], w_ref[...], preferred_element_type=F32)


def _norm_matmul(x, g, w, *, tm, tn):
    m, k = x.shape
    n = w.shape[1]
    tm, tn = _tile(m, tm), _tile(n, tn)
    return pl.pallas_call(
        _norm_matmul_kernel,
        grid=(m // tm, n // tn),
        in_specs=[pl.BlockSpec((tm, k), lambda i, j: (i, 0)),
                  pl.BlockSpec((1, k), lambda i, j: (0, 0)),
                  pl.BlockSpec((k, tn), lambda i, j: (0, j))],
        out_specs=pl.BlockSpec((tm, tn), lambda i, j: (i, j)),
        out_shape=jax.ShapeDtypeStruct((m, n), F32),
        scratch_shapes=[pltpu.VMEM((tm, k), MXU_DTYPE)],
        compiler_params=_cparams("parallel", "arbitrary"),
        name="norm_matmul",
    )(x, g.reshape(1, k), w)


def _ffn_kernel(*refs, has_tail, has_final):
    x_ref, g_ref, wa_ref, wb_ref, wo_ref = refs[:5]
    rest = list(refs[5:])
    tail_refs = [rest.pop(0) for _ in range(3)] if has_tail else None
    final_g_ref = rest.pop(0) if has_final else None
    o_ref, xn_ref, acc_ref = rest
    j = pl.program_id(1)

    @pl.when(j == 0)
    def _():
        xn_ref[...] = _rms(x_ref[...], g_ref[...]).astype(xn_ref.dtype)
        acc_ref[...] = jnp.zeros_like(acc_ref)

    def swiglu(wa, wb, wo):
        xn = xn_ref[...]
        a = jnp.dot(xn, wa[...], preferred_element_type=F32)
        b = jnp.dot(xn, wb[...], preferred_element_type=F32)
        return _mm(_silu(a) * b, wo[...])

    acc_ref[...] += swiglu(wa_ref, wb_ref, wo_ref)

    @pl.when(j == pl.num_programs(1) - 1)
    def _():
        acc = acc_ref[...]
        if has_tail:
            acc = acc + swiglu(*tail_refs)
        y = x_ref[...] + 0.5 * acc
        if has_final:
            y = _rms(y, final_g_ref[...])
        o_ref[...] = y


def _ffn(x, g, weights, *, layer, slot, tm, tf, final_g=None):
    w_in, w_b, w_out, tails = weights
    m, d = x.shape
    f = w_out.shape[2]
    tm = min(tm, m)
    assert m % tm == 0 and f >= tf
    pick = lambda r, c: pl.BlockSpec((None, None, r, c), lambda i, j: (layer, slot, 0, 0))
    in_specs = [pl.BlockSpec((tm, d), lambda i, j: (i, 0)),
                pl.BlockSpec((1, d), lambda i, j: (0, 0)),
                pl.BlockSpec((None, None, d, tf), lambda i, j: (layer, slot, 0, j)),
                pl.BlockSpec((None, None, d, tf), lambda i, j: (layer, slot, 0, j)),
                pl.BlockSpec((None, None, tf, d), lambda i, j: (layer, slot, j, 0))]
    args = [x, g.reshape(1, d), w_in, w_b, w_out]
    if tails is not None:
        rem = f % tf
        in_specs += [pick(d, rem), pick(d, rem), pick(rem, d)]
        args += list(tails)
    if final_g is not None:
        in_specs.append(pl.BlockSpec((1, d), lambda i, j: (0, 0)))
        args.append(final_g.reshape(1, d))
    return pl.pallas_call(
        functools.partial(_ffn_kernel, has_tail=tails is not None, has_final=final_g is not None),
        grid=(m // tm, f // tf),
        in_specs=in_specs,
        out_specs=pl.BlockSpec((tm, d), lambda i, j: (i, 0)),
        out_shape=jax.ShapeDtypeStruct((m, d), F32),
        scratch_shapes=[pltpu.VMEM((tm, d), MXU_DTYPE), pltpu.VMEM((tm, d), F32)],
        compiler_params=_cparams("parallel", "arbitrary"),
        name="ffn",
    )(*args)


def _ffn_prep(ffn_w_in, ffn_w_out, tf):
    f = ffn_w_out.shape[2]
    w_in = ffn_w_in.astype(MXU_DTYPE)
    w_out = ffn_w_out.astype(MXU_DTYPE)
    w_b = w_in[..., f:]
    main = (f // tf) * tf
    tails = None if main == f else (w_in[..., main:f], w_b[..., main:], w_out[:, :, main:, :])
    return w_in, w_b, w_out, tails


def _matmul_residual_kernel(*refs, n_in):
    x_ref, o_ref = refs[0], refs[-1]
    acc = x_ref[...]
    for i in range(n_in):
        acc = acc + _mm(refs[1 + i][...], refs[1 + n_in + i][...])
    o_ref[...] = acc


def _matmul_residual(x, ys, ws, *, tm, tn):
    m, n = x.shape
    tm, tn = _tile(m, tm), _tile(n, tn)
    n_in = len(ys)
    in_specs = [pl.BlockSpec((tm, tn), lambda i, j: (i, j))]
    in_specs += [pl.BlockSpec((tm, y.shape[1]), lambda i, j: (i, 0)) for y in ys]
    in_specs += [pl.BlockSpec((w.shape[0], tn), lambda i, j: (0, j)) for w in ws]
    return pl.pallas_call(
        functools.partial(_matmul_residual_kernel, n_in=n_in),
        grid=(m // tm, n // tn),
        in_specs=in_specs,
        out_specs=pl.BlockSpec((tm, tn), lambda i, j: (i, j)),
        out_shape=jax.ShapeDtypeStruct((m, n), F32),
        compiler_params=_cparams("parallel", "arbitrary"),
        name="matmul_residual",
    )(x, *ys, *ws)


def _moba_kernel(q_ref, k_ref, v_ref, o_ref, kb_ref, vt_ref, kmean_ref, sel_ref, *, seq, group):
    qi = pl.program_id(2)
    nb = seq // MOBA_BLOCK
    rows = MOBA_BLOCK
    grp = range(group)
    cols = [slice(j * HEAD_DIM, (j + 1) * HEAD_DIM) for j in grp]

    @pl.when(qi == 0)
    def _():
        for j in grp:
            for n in range(nb):
                kn = k_ref[n * rows:(n + 1) * rows, cols[j]]
                kb_ref[j, n] = kn.astype(kb_ref.dtype)
                kmean_ref[j, n:n + 1, :] = jnp.mean(kn, axis=0, keepdims=True)
                vt_ref[j, n] = v_ref[n * rows:(n + 1) * rows, cols[j]].T.astype(vt_ref.dtype)

    q = [q_ref[:, cols[j]] for j in grp]
    blk = lax.broadcasted_iota(jnp.int32, (nb, rows), 0)
    past = blk < qi
    gate = [jnp.where(past, lax.dot_general(kmean_ref[j], q[j], (((1,), (1,)), ((), ())), precision=HIGHEST,
                                            preferred_element_type=F32), NEG_INF) for j in grp]
    rank = [jnp.zeros((nb, rows), jnp.int32) for _ in grp]
    for m in range(nb):
        for j in grp:
            gm = gate[j][m:m + 1, :]
            beats = (gm > gate[j]) | ((gm == gate[j]) & (m < blk))
            rank[j] = rank[j] + beats.astype(jnp.int32)
    for j in grp:
        sel_ref[j] = jnp.where(past & (rank[j] < MOBA_TOPK), 1.0, 0.0)

    scale = HEAD_DIM ** -0.5
    qb = [q[j].astype(kb_ref.dtype) for j in grp]
    key = lax.broadcasted_iota(jnp.int32, (rows, rows), 0)
    qry = lax.broadcasted_iota(jnp.int32, (rows, rows), 1)
    s = [jnp.where(key <= qry, _mm_nt(kb_ref[j, qi], qb[j]) * scale, NEG_INF) for j in grp]
    m0 = [jnp.max(s[j], axis=0, keepdims=True) for j in grp]
    p = [jnp.exp(s[j] - m0[j]) for j in grp]
    l0 = [jnp.sum(p[j], axis=0, keepdims=True) for j in grp]
    acc0 = [_mm(vt_ref[j, qi], p[j]) for j in grp]

    def body(i, carry):
        m_run, l_run, acc = carry
        n0 = MOBA_STEP * i
        ss = [[jnp.where(sel_ref[j, pl.ds(n0 + d, 1), :] > 0.0, _mm_nt(kb_ref[j, n0 + d], qb[j]) * scale, NEG_INF)
               for d in range(MOBA_STEP)] for j in grp]
        m_new = list(m_run)
        for j in grp:
            for sj in ss[j]:
                m_new[j] = jnp.maximum(m_new[j], jnp.max(sj, axis=0, keepdims=True))
        alpha = [jnp.exp(m_run[j] - m_new[j]) for j in grp]
        ps = [[jnp.exp(sj - m_new[j]) for sj in ss[j]] for j in grp]
        l_new = [alpha[j] * l_run[j] for j in grp]
        acc = [alpha[j] * acc[j] for j in grp]
        for d in range(MOBA_STEP):
            for j in grp:
                l_new[j] = l_new[j] + jnp.sum(ps[j][d], axis=0, keepdims=True)
                acc[j] = acc[j] + _mm(vt_ref[j, n0 + d], ps[j][d])
        return tuple(m_new), tuple(l_new), tuple(acc)

    trips = (qi + MOBA_STEP - 1) // MOBA_STEP
    _, l_fin, acc = lax.fori_loop(0, trips, body, (tuple(m0), tuple(l0), tuple(acc0)))
    for j in grp:
        o_ref[:, cols[j]] = (acc[j] / l_fin[j]).T


def _moba(proj, *, heads):
    b, t, _ = proj.shape
    nb = t // MOBA_BLOCK
    group = min(MOBA_GROUP, heads)
    assert t % MOBA_BLOCK == 0 and nb % MOBA_STEP == 0 and heads % group == 0
    width = group * HEAD_DIM
    ngrp = heads // group
    return pl.pallas_call(
        functools.partial(_moba_kernel, seq=t, group=group),
        grid=(b, ngrp, nb),
        in_specs=[pl.BlockSpec((None, MOBA_BLOCK, width), lambda bi, g, i: (bi, i, g)),
                  pl.BlockSpec((None, t, width), lambda bi, g, i: (bi, 0, ngrp + g)),
                  pl.BlockSpec((None, t, width), lambda bi, g, i: (bi, 0, 2 * ngrp + g))],
        out_specs=pl.BlockSpec((None, MOBA_BLOCK, width), lambda bi, g, i: (bi, i, g)),
        out_shape=jax.ShapeDtypeStruct((b, t, heads * HEAD_DIM), F32),
        scratch_shapes=[pltpu.VMEM((group, nb, MOBA_BLOCK, HEAD_DIM), MXU_DTYPE),
                        pltpu.VMEM((group, nb, HEAD_DIM, MOBA_BLOCK), MXU_DTYPE),
                        pltpu.VMEM((group, nb, HEAD_DIM), F32),
                        pltpu.VMEM((group, nb, MOBA_BLOCK), F32)],
        compiler_params=_cparams("parallel", "parallel", "arbitrary"),
        name="moba",
    )(proj, proj, proj)


def _chunk_masks():
    row = lax.broadcasted_iota(jnp.int32, (CHUNK, CHUNK), 0)
    col = lax.broadcasted_iota(jnp.int32, (CHUNK, CHUNK), 1)
    return row >= col, row > col, (row == col).astype(F32)


def _conv_silu(prev, x, cw):
    xe = jnp.concatenate([prev, x], axis=0)
    acc = x * cw[CONV_W - 1:CONV_W, :]
    for k in range(1, CONV_W):
        acc = acc + pltpu.roll(xe, k, axis=0)[CONV_HALO:, :] * cw[CONV_W - 1 - k:CONV_W - k, :]
    return _silu(acc)


def _unit_lower_inverse(m_mats, eye):
    pows = [[-m] for m in m_mats]
    for _ in range(CHUNK.bit_length() - 2):
        for p in pows:
            p.append(_mm_x3(p[-1], p[-1]))
    terms = [[eye + pw for pw in p] for p in pows]
    while len(terms[0]) > 1:
        terms = [[_mm_x3(t[i], t[i + 1]) for i in range(0, len(t) - 1, 2)] + t[len(t) - len(t) % 2:]
                 for t in terms]
    return [t[0] for t in terms]


def _deltanet_kernel(q_ref, k_ref, v_ref, gate_ref, small_ref, cwq_ref, cwk_ref, cwv_ref,
                     alog_ref, dt_ref, ong_ref, o_ref, state_ref, halo_ref, *, heads, group, tile):
    hg = pl.program_id(1)
    tril, strict, eye = _chunk_masks()
    ones_tril = tril.astype(F32)
    lane = lax.broadcasted_iota(jnp.int32, (CHUNK, HEAD_DIM), 1)
    ong = ong_ref[...]

    @pl.when(pl.program_id(2) == 0)
    def _():
        state_ref[...] = jnp.zeros_like(state_ref)
        halo_ref[...] = jnp.zeros_like(halo_ref)

    grp = range(group)
    cols = [slice(j * HEAD_DIM, (j + 1) * HEAD_DIM) for j in grp]

    def body(c, carry):
        rows = pl.ds(pl.multiple_of(c * CHUNK, CHUNK), CHUNK)
        sm = small_ref[rows, :]
        dq, dk, dv, beta, g = [], [], [], [], []
        for j in grp:
            head = hg * group + j
            xq, xk, xv = q_ref[rows, cols[j]], k_ref[rows, cols[j]], v_ref[rows, cols[j]]
            cq = _conv_silu(halo_ref[3 * j], xq, cwq_ref[:, cols[j]])
            ck = _conv_silu(halo_ref[3 * j + 1], xk, cwk_ref[:, cols[j]])
            dv.append(_conv_silu(halo_ref[3 * j + 2], xv, cwv_ref[:, cols[j]]))
            halo_ref[3 * j] = xq[CHUNK - CONV_HALO:, :]
            halo_ref[3 * j + 1] = xk[CHUNK - CONV_HALO:, :]
            halo_ref[3 * j + 2] = xv[CHUNK - CONV_HALO:, :]
            dq.append(cq * lax.rsqrt(jnp.sum(cq * cq, axis=-1, keepdims=True) + EPS) * (HEAD_DIM ** -0.5))
            dk.append(ck * lax.rsqrt(jnp.sum(ck * ck, axis=-1, keepdims=True) + EPS))
            beta_raw = jnp.sum(jnp.where(lane == head, sm, 0.0), axis=1, keepdims=True)
            alpha_raw = jnp.sum(jnp.where(lane == heads + head, sm, 0.0), axis=1, keepdims=True)
            beta.append(_sigmoid(beta_raw))
            z = alpha_raw + dt_ref[pl.ds(head, 1), :]
            g.append(-jnp.exp(alog_ref[pl.ds(head, 1), :])
                     * (jnp.maximum(z, 0.0) + jnp.log1p(jnp.exp(-jnp.abs(z)))))
        gc = [_cumsum_rows(ones_tril, g[j]) for j in grp]
        gamma, e_g, kb = [], [], []
        for j in grp:
            g_col = gc[j][:, :CHUNK]
            g_row = jnp.sum(g_col * eye, axis=0, keepdims=True)
            gamma.append(jnp.exp(jnp.where(tril, g_col - g_row, NEG_INF)))
            e_g.append(jnp.exp(gc[j]))
            kb.append(dk[j] * beta[j])
        kq = [_mm_nt(jnp.concatenate([kb[j], dq[j]], axis=0), dk[j]) for j in grp]
        t_inv = _unit_lower_inverse([jnp.where(strict, kq[j][:CHUNK] * gamma[j], 0.0) for j in grp], eye)
        a_qk = [kq[j][CHUNK:] * gamma[j] for j in grp]
        uw = [_mm(t_inv[j], jnp.concatenate([dv[j] * beta[j], kb[j] * e_g[j]], axis=1)) for j in grp]
        state = [state_ref[j] for j in grp]
        ws = [_mm(jnp.concatenate([uw[j][:, HEAD_DIM:], dq[j] * e_g[j]], axis=0), state[j]) for j in grp]
        v_new = [uw[j][:, :HEAD_DIM] - ws[j][:CHUNK] for j in grp]
        o = [ws[j][CHUNK:] + _mm(a_qk[j], v_new[j]) for j in grp]
        for j in grp:
            g_last = gc[j][CHUNK - 1:CHUNK, :]
            state_ref[j] = jnp.exp(g_last) * state[j] + _mm_tn(dk[j] * jnp.exp(g_last - gc[j]), v_new[j])
        for j in grp:
            o_ref[rows, cols[j]] = _rms(o[j], ong) * _silu(gate_ref[rows, cols[j]])
        return carry

    lax.fori_loop(0, tile // CHUNK, body, 0)


def _deltanet(proj, small, conv_w, a_log, dt_bias, o_norm_g, *, col0, heads):
    b, t, _ = proj.shape
    group = min(DN_GROUP, heads)
    tile = min(SEQ_TILE, t)
    assert t % tile == 0 and tile % CHUNK == 0 and heads % group == 0 and col0 % group == 0
    width = group * HEAD_DIM

    def col(off):
        base = (col0 + off * heads) // group
        return pl.BlockSpec((None, tile, width), lambda bi, hg, ti: (bi, ti, base + hg))

    def cw(off):
        base = (off * heads) // group
        return pl.BlockSpec((CONV_W, width), lambda bi, hg, ti: (0, base + hg))

    rep = lambda a: jnp.broadcast_to(a.astype(F32)[:, None], (heads, HEAD_DIM))
    whole = lambda r: pl.BlockSpec((r, HEAD_DIM), lambda bi, hg, ti: (0, 0))
    return pl.pallas_call(
        functools.partial(_deltanet_kernel, heads=heads, group=group, tile=tile),
        grid=(b, heads // group, t // tile),
        in_specs=[col(0), col(1), col(2), col(3),
                  pl.BlockSpec((None, tile, HEAD_DIM), lambda bi, hg, ti: (bi, ti, 0)),
                  cw(0), cw(1), cw(2), whole(heads), whole(heads), whole(1)],
        out_specs=pl.BlockSpec((None, tile, width), lambda bi, hg, ti: (bi, ti, hg)),
        out_shape=jax.ShapeDtypeStruct((b, t, heads * HEAD_DIM), F32),
        scratch_shapes=[pltpu.VMEM((group, HEAD_DIM, HEAD_DIM), F32),
                        pltpu.VMEM((3 * group, CONV_HALO, HEAD_DIM), F32)],
        compiler_params=_cparams("parallel", "parallel", "arbitrary"),
        name="deltanet",
    )(proj, proj, proj, proj, small, conv_w, conv_w, conv_w, rep(a_log), rep(dt_bias),
      o_norm_g.reshape(1, HEAD_DIM))


def _hgrn2_kernel(q_ref, f_ref, i_ref, g_ref, lbl_ref, ong_ref, o_ref, state_ref, *, layer, group, tile):
    tril, _, _ = _chunk_masks()
    ones_tril = tril.astype(F32)
    lbl = lbl_ref[...]
    e = jnp.exp(lbl - jnp.max(lbl, axis=0, keepdims=True))
    p = e / jnp.sum(e, axis=0, keepdims=True)
    cs = p[0:1, :]
    for r in range(1, layer + 1):
        cs = cs + p[r:r + 1, :]
    lb_all = cs - p[0:1, :]
    ong = ong_ref[...]
    pos = lax.broadcasted_iota(jnp.int32, (CHUNK, 1), 0) % HALF
    same_sub = (lax.broadcasted_iota(jnp.int32, (CHUNK // 2, CHUNK // 2), 0) // HALF
                == lax.broadcasted_iota(jnp.int32, (CHUNK // 2, CHUNK // 2), 1) // HALF)

    @pl.when(pl.program_id(2) == 0)
    def _():
        state_ref[...] = jnp.zeros_like(state_ref)

    grp = range(group)
    cols = [slice(j * HEAD_DIM, (j + 1) * HEAD_DIM) for j in grp]

    def body(c, carry):
        rows = pl.ds(pl.multiple_of(c * CHUNK, CHUNK), CHUNK)
        q, k, v, log_f = [], [], [], []
        for j in grp:
            lb = lb_all[:, cols[j]]
            fr = f_ref[rows, cols[j]]
            log_f.append(jnp.log(lb + (1.0 - lb) * _sigmoid(fr)))
            k.append((1.0 - lb) * _sigmoid(-fr))
            q.append(_silu(q_ref[rows, cols[j]]) * (HEAD_DIM ** -0.5))
            v.append(i_ref[rows, cols[j]])
        b = [_cumsum_rows(ones_tril, log_f[j]) for j in grp]
        state = [state_ref[j] for j in grp]
        o = [_mm_nt(q[j] * jnp.exp(b[j]), state[j]) for j in grp]
        scores = []
        for i in range(1, CHUNK // SUB):
            lo = i * SUB
            for j in grp:
                ref_row = b[j][lo:lo + 1, :]
                qt = q[j][lo:lo + SUB, :] * jnp.exp(b[j][lo:lo + SUB, :] - ref_row)
                kt = k[j][:lo, :] * jnp.exp(ref_row - b[j][:lo, :])
                scores.append(_mm_nt(qt, kt))
        below = [[jnp.zeros((SUB, HEAD_DIM), F32)] for _ in grp]
        for i in range(1, CHUNK // SUB):
            for j in grp:
                below[j].append(_mm(scores[(i - 1) * group + j], v[j][:i * SUB, :]))
        qh, kh, vh = [], [], []
        for j in grp:
            qt, kt, vt = [], [], []
            for i in range(CHUNK // SUB):
                lo, mid = i * SUB, i * SUB + HALF
                ref_row = b[j][mid:mid + 1, :]
                qt.append(q[j][mid:mid + HALF, :] * jnp.exp(b[j][mid:mid + HALF, :] - ref_row))
                kt.append(k[j][lo:mid, :] * jnp.exp(ref_row - b[j][lo:mid, :]))
                vt.append(v[j][lo:mid, :])
            qh.append(jnp.concatenate(qt, axis=0))
            kh.append(jnp.concatenate(kt, axis=0))
            vh.append(jnp.concatenate(vt, axis=0))
        half_scores = [jnp.where(same_sub, _mm_nt(qh[j], kh[j]), 0.0) for j in grp]
        half_out = [_mm(half_scores[j], vh[j]) for j in grp]
        zero_half = jnp.zeros((HALF, HEAD_DIM), F32)
        for j in grp:
            spread = []
            for i in range(CHUNK // SUB):
                spread += [zero_half, half_out[j][i * HALF:(i + 1) * HALF, :]]
            o[j] = o[j] + jnp.concatenate(below[j], axis=0) + jnp.concatenate(spread, axis=0)
        for d in range(HALF):
            for j in grp:
                kr, br, vr = ((k[j], b[j], v[j]) if d == 0 else
                              tuple(pltpu.roll(a, d, axis=0) for a in (k[j], b[j], v[j])))
                decay = jnp.exp(jnp.where(pos >= d, b[j] - br, NEG_INF))
                o[j] = o[j] + jnp.sum(q[j] * kr * decay, axis=1, keepdims=True) * vr
        for j in grp:
            b_last = b[j][CHUNK - 1:CHUNK, :]
            state_ref[j] = state[j] * jnp.exp(b_last) + _mm_tn(v[j], k[j] * jnp.exp(b_last - b[j]))
        for j in grp:
            o_ref[rows, cols[j]] = _rms(o[j], ong) * _silu(g_ref[rows, cols[j]])
        return carry

    lax.fori_loop(0, tile // CHUNK, body, 0)


def _hgrn2(proj, lb_logits, o_norm_g, *, layer, heads):
    b, t, _ = proj.shape
    depth = lb_logits.shape[0]
    group = min(HEAD_GROUP, heads)
    tile = min(SEQ_TILE, t)
    assert t % tile == 0 and tile % CHUNK == 0 and heads % group == 0
    width = group * HEAD_DIM

    def col(off):
        base = (off * heads) // group
        return pl.BlockSpec((None, tile, width), lambda bi, hg, ti: (bi, ti, base + hg))

    return pl.pallas_call(
        functools.partial(_hgrn2_kernel, layer=layer, group=group, tile=tile),
        grid=(b, heads // group, t // tile),
        in_specs=[col(0), col(1), col(2), col(3),
                  pl.BlockSpec((depth, width), lambda bi, hg, ti: (0, hg)),
                  pl.BlockSpec((1, HEAD_DIM), lambda bi, hg, ti: (0, 0))],
        out_specs=pl.BlockSpec((None, tile, width), lambda bi, hg, ti: (bi, ti, hg)),
        out_shape=jax.ShapeDtypeStruct((b, t, heads * HEAD_DIM), F32),
        scratch_shapes=[pltpu.VMEM((group, HEAD_DIM, HEAD_DIM), F32)],
        compiler_params=_cparams("parallel", "parallel", "arbitrary"),
        name="hgrn2",
    )(proj, proj, proj, proj, lb_logits.astype(F32), o_norm_g.reshape(1, HEAD_DIM))


def _cross_attn_kernel(x_ref, g_ref, wq_ref, kv_ref, wo_ref, o_ref):
    x = x_ref[...]
    xn = _rms(x, g_ref[...])
    q = _mm(xn, wq_ref[...])
    width = X_HEADS * HEAD_DIM
    scale = HEAD_DIM ** -0.5
    outs = []
    for h in range(X_HEADS):
        lo = h * HEAD_DIM
        s = _mm_nt(q[:, lo:lo + HEAD_DIM], kv_ref[:, lo:lo + HEAD_DIM]) * scale
        p = jnp.exp(s - jnp.max(s, axis=-1, keepdims=True))
        p = p / jnp.sum(p, axis=-1, keepdims=True)
        outs.append(_mm(p, kv_ref[:, width + lo:width + lo + HEAD_DIM]))
    o_ref[...] = x + _mm(jnp.concatenate(outs, axis=-1), wo_ref[...])


def _cross_attn(x, g, wq, kv, wo, *, layer, seq, tm):
    m, d = x.shape
    mem_tokens = kv.shape[0] // (m // seq)
    tm = min(tm, seq)
    assert seq % tm == 0
    per_seq = seq // tm
    width = X_HEADS * HEAD_DIM
    return pl.pallas_call(
        _cross_attn_kernel,
        grid=(m // tm,),
        in_specs=[pl.BlockSpec((tm, d), lambda i: (i, 0)),
                  pl.BlockSpec((1, d), lambda i: (0, 0)),
                  pl.BlockSpec((d, width), lambda i: (0, 0)),
                  pl.BlockSpec((mem_tokens, 2 * width), lambda i: (i // per_seq, layer)),
                  pl.BlockSpec((width, d), lambda i: (0, 0))],
        out_specs=pl.BlockSpec((tm, d), lambda i: (i, 0)),
        out_shape=jax.ShapeDtypeStruct((m, d), F32),
        compiler_params=_cparams("parallel"),
        name="cross_attn",
    )(x, g.reshape(1, d), wq, kv, wo)


TM = 512
TM_PROJ = 1024
TN = 1024
TF = 512


def _moba_deltanet_mixer(x, g, w_in, conv_w, a_log, dt_bias, o_norm_g, w_out, *, batch):
    m, d = x.shape
    seq = m // batch
    dn_heads = a_log.shape[0]
    main = w_in.shape[1] - 2 * dn_heads
    dn_width = dn_heads * HEAD_DIM
    moba_heads = (main - 4 * dn_width) // (3 * HEAD_DIM)
    proj = _norm_matmul(x, g, w_in[:, :main].astype(MXU_DTYPE), tm=TM_PROJ, tn=TN)
    w_small = jnp.pad(w_in[:, main:], ((0, 0), (0, HEAD_DIM - 2 * dn_heads))).astype(MXU_DTYPE)
    small = _norm_matmul(x, g, w_small, tm=TM, tn=HEAD_DIM)
    proj = proj.reshape(batch, seq, main)
    small = small.reshape(batch, seq, HEAD_DIM)
    y_a = _moba(proj, heads=moba_heads)
    y_b = _deltanet(proj, small, conv_w, a_log, dt_bias, o_norm_g, col0=3 * moba_heads, heads=dn_heads)
    wa = w_out[:moba_heads * HEAD_DIM].astype(MXU_DTYPE)
    wb = w_out[moba_heads * HEAD_DIM:].astype(MXU_DTYPE)
    return _matmul_residual(x, [y_a.reshape(m, -1), y_b.reshape(m, -1)], [wa, wb], tm=TM, tn=TN)


def _hgrn2_mixer(x, g, w_in, lb_logits, o_norm_g, w_out, *, batch, layer):
    m, d = x.shape
    seq = m // batch
    width = w_in.shape[1] // 4
    proj = _norm_matmul(x, g, w_in.astype(MXU_DTYPE), tm=TM_PROJ, tn=TN).reshape(batch, seq, 4 * width)
    y = _hgrn2(proj, lb_logits, o_norm_g, layer=layer, heads=width // HEAD_DIM)
    return _matmul_residual(x, [y.reshape(m, width)], [w_out.astype(MXU_DTYPE)], tm=TM, tn=TN)


def kernel(x, mem, norm_g, mem_norm_g, final_norm_g, ffn_w_in, ffn_w_out, ab_w_in, ab_conv_w, ab_a_log, ab_dt_bias, ab_o_norm_g, ab_w_out, c_w_in, c_lb_logits, c_o_norm_g, c_w_out, x_w_q, x_w_kv, x_w_o):
    batch, seq, d = x.shape
    depth = norm_g.shape[0]
    m = batch * seq
    xf = x.reshape(m, d).astype(F32)
    w_kv = jnp.transpose(x_w_kv, (1, 0, 2)).reshape(d, -1).astype(MXU_DTYPE)
    kv = _norm_matmul(mem.reshape(-1, d).astype(F32), mem_norm_g, w_kv, tm=TM, tn=TN)
    ffn_weights = _ffn_prep(ffn_w_in, ffn_w_out, TF)
    for l in range(depth):
        xf = _ffn(xf, norm_g[l, 0], ffn_weights, layer=l, slot=0, tm=TM, tf=TF)
        if l % 2 == 0:
            e = l // 2
            xf = _moba_deltanet_mixer(xf, norm_g[l, 1], ab_w_in[e], ab_conv_w[e], ab_a_log[e], ab_dt_bias[e],
                                      ab_o_norm_g[e], ab_w_out[e], batch=batch)
        else:
            o = l // 2
            xf = _hgrn2_mixer(xf, norm_g[l, 1], c_w_in[o], c_lb_logits, c_o_norm_g[o], c_w_out[o],
                              batch=batch, layer=l)
        xf = _cross_attn(xf, norm_g[l, 2], x_w_q[l].astype(MXU_DTYPE), kv, x_w_o[l].astype(MXU_DTYPE),
                         layer=l, seq=seq, tm=TM)
        xf = _ffn(xf, norm_g[l, 3], ffn_weights, layer=l, slot=1, tm=TM, tf=TF,
                  final_g=final_norm_g if l == depth - 1 else None)
    return xf.reshape(batch, seq, d)
```

```python
import functools

import jax
import jax.numpy as jnp
from jax import lax
from jax.experimental import pallas as pl
from jax.experimental.pallas import tpu as pltpu

F32 = jnp.float32
MXU_DTYPE = jnp.bfloat16
HIGHEST = lax.Precision.HIGHEST
EPS = 1e-6
NEG_INF = float("-inf")

HEAD_DIM = 128
MOBA_BLOCK = 256
MOBA_TOPK = 3
MOBA_STEP = 2
CHUNK = 64
SUB = 16
HALF = SUB // 2
CONV_W = 4
CONV_HALO = 8
X_HEADS = 4
MOBA_GROUP = 4
DN_GROUP = 8
HEAD_GROUP = 4
SEQ_TILE = 512
VMEM_LIMIT = 56 * 1024 * 1024


def _cparams(*sem):
    return pltpu.CompilerParams(dimension_semantics=sem, vmem_limit_bytes=VMEM_LIMIT)


def _tile(n, t):
    if n <= t:
        return n
    t -= t % HEAD_DIM
    while n % t:
        t -= HEAD_DIM
    return t


def _mm(a, b):
    return jnp.dot(a.astype(MXU_DTYPE), b.astype(MXU_DTYPE), preferred_element_type=F32)


def _mm_nt(a, b):
    return lax.dot_general(a.astype(MXU_DTYPE), b.astype(MXU_DTYPE), (((1,), (1,)), ((), ())),
                           preferred_element_type=F32)


def _mm_tn(a, b):
    return lax.dot_general(a.astype(MXU_DTYPE), b.astype(MXU_DTYPE), (((0,), (0,)), ((), ())),
                           preferred_element_type=F32)


def _split2(x):
    hi = x.astype(MXU_DTYPE)
    return hi, (x - hi.astype(F32)).astype(MXU_DTYPE)


def _mm_x3(a, b):
    ah, al = _split2(a)
    bh, bl = _split2(b)
    dot = functools.partial(jnp.dot, preferred_element_type=F32)
    return dot(ah, bh) + (dot(ah, bl) + dot(al, bh))


def _cumsum_rows(ones_tril, x):
    hi = x.astype(MXU_DTYPE)
    rest = x - hi.astype(F32)
    mid = rest.astype(MXU_DTYPE)
    lo = (rest - mid.astype(F32)).astype(MXU_DTYPE)
    dot = functools.partial(jnp.dot, ones_tril.astype(MXU_DTYPE), preferred_element_type=F32)
    return dot(hi) + (dot(mid) + dot(lo))


def _sigmoid(x):
    return 1.0 / (1.0 + jnp.exp(-x))


def _silu(x):
    return x * _sigmoid(x)


def _rms(x, g):
    return x * lax.rsqrt(jnp.mean(x * x, axis=-1, keepdims=True) + EPS) * g


def _norm_matmul_kernel(x_ref, g_ref, w_ref, o_ref, xn_ref):
    @pl.when(pl.program_id(1) == 0)
    def _():
        xn_ref[...] = _rms(x_ref[...], g_ref[...]).astype(xn_ref.dtype)

    o_ref[...] = jnp.dot(xn_ref[...], w_ref[...], preferred_element_type=F32)


def _norm_matmul(x, g, w, *, tm, tn):
    m, k = x.shape
    n = w.shape[1]
    tm, tn = _tile(m, tm), _tile(n, tn)
    return pl.pallas_call(
        _norm_matmul_kernel,
        grid=(m // tm, n // tn),
        in_specs=[pl.BlockSpec((tm, k), lambda i, j: (i, 0)),
                  pl.BlockSpec((1, k), lambda i, j: (0, 0)),
                  pl.BlockSpec((k, tn), lambda i, j: (0, j))],
        out_specs=pl.BlockSpec((tm, tn), lambda i, j: (i, j)),
        out_shape=jax.ShapeDtypeStruct((m, n), F32),
        scratch_shapes=[pltpu.VMEM((tm, k), MXU_DTYPE)],
        compiler_params=_cparams("parallel", "arbitrary"),
        name="norm_matmul",
    )(x, g.reshape(1, k), w)


def _ffn_kernel(*refs, has_tail, has_final):
    x_ref, g_ref, wa_ref, wb_ref, wo_ref = refs[:5]
    rest = list(refs[5:])
    tail_refs = [rest.pop(0) for _ in range(3)] if has_tail else None
    final_g_ref = rest.pop(0) if has_final else None
    o_ref, xn_ref, acc_ref = rest
    j = pl.program_id(1)

    @pl.when(j == 0)
    def _():
        xn_ref[...] = _rms(x_ref[...], g_ref[...]).astype(xn_ref.dtype)
        acc_ref[...] = jnp.zeros_like(acc_ref)

    def swiglu(wa, wb, wo):
        xn = xn_ref[...]
        a = jnp.dot(xn, wa[...], preferred_element_type=F32)
        b = jnp.dot(xn, wb[...], preferred_element_type=F32)
        return _mm(_silu(a) * b, wo[...])

    acc_ref[...] += swiglu(wa_ref, wb_ref, wo_ref)

    @pl.when(j == pl.num_programs(1) - 1)
    def _():
        acc = acc_ref[...]
        if has_tail:
            acc = acc + swiglu(*tail_refs)
        y = x_ref[...] + 0.5 * acc
        if has_final:
            y = _rms(y, final_g_ref[...])
        o_ref[...] = y


def _ffn(x, g, weights, *, layer, slot, tm, tf, final_g=None):
    w_a, w_b, w_out, tails = weights
    m, d = x.shape
    f = w_out.shape[2]
    tm = min(tm, m)
    assert m % tm == 0 and f >= tf
    pick = lambda r, c: pl.BlockSpec((None, None, r, c), lambda i, j: (layer, slot, 0, 0))
    in_specs = [pl.BlockSpec((tm, d), lambda i, j: (i, 0)),
                pl.BlockSpec((1, d), lambda i, j: (0, 0)),
                pl.BlockSpec((None, None, d, tf), lambda i, j: (layer, slot, 0, j)),
                pl.BlockSpec((None, None, d, tf), lambda i, j: (layer, slot, 0, j)),
                pl.BlockSpec((None, None, tf, d), lambda i, j: (layer, slot, j, 0))]
    args = [x, g.reshape(1, d), w_a, w_b, w_out]
    if tails is not None:
        rem = f % tf
        in_specs += [pick(d, rem), pick(d, rem), pick(rem, d)]
        args += list(tails)
    if final_g is not None:
        in_specs.append(pl.BlockSpec((1, d), lambda i, j: (0, 0)))
        args.append(final_g.reshape(1, d))
    return pl.pallas_call(
        functools.partial(_ffn_kernel, has_tail=tails is not None, has_final=final_g is not None),
        grid=(m // tm, f // tf),
        in_specs=in_specs,
        out_specs=pl.BlockSpec((tm, d), lambda i, j: (i, 0)),
        out_shape=jax.ShapeDtypeStruct((m, d), F32),
        scratch_shapes=[pltpu.VMEM((tm, d), MXU_DTYPE), pltpu.VMEM((tm, d), F32)],
        compiler_params=_cparams("parallel", "arbitrary"),
        name="ffn",
    )(*args)


def _ffn_prep(ffn_w_in, ffn_w_out, tf):
    f = ffn_w_out.shape[2]
    w_a = ffn_w_in[..., :f].astype(MXU_DTYPE)
    w_b = ffn_w_in[..., f:].astype(MXU_DTYPE)
    w_out = ffn_w_out.astype(MXU_DTYPE)
    main = (f // tf) * tf
    tails = None if main == f else (w_a[..., main:], w_b[..., main:], w_out[:, :, main:, :])
    return w_a, w_b, w_out, tails


def _matmul_residual_kernel(*refs, n_in):
    x_ref, o_ref = refs[0], refs[-1]
    acc = x_ref[...]
    for i in range(n_in):
        acc = acc + _mm(refs[1 + i][...], refs[1 + n_in + i][...])
    o_ref[...] = acc


def _matmul_residual(x, ys, ws, *, tm, tn):
    m, n = x.shape
    tm, tn = _tile(m, tm), _tile(n, tn)
    n_in = len(ys)
    in_specs = [pl.BlockSpec((tm, tn), lambda i, j: (i, j))]
    in_specs += [pl.BlockSpec((tm, y.shape[1]), lambda i, j: (i, 0)) for y in ys]
    in_specs += [pl.BlockSpec((w.shape[0], tn), lambda i, j: (0, j)) for w in ws]
    return pl.pallas_call(
        functools.partial(_matmul_residual_kernel, n_in=n_in),
        grid=(m // tm, n // tn),
        in_specs=in_specs,
        out_specs=pl.BlockSpec((tm, tn), lambda i, j: (i, j)),
        out_shape=jax.ShapeDtypeStruct((m, n), F32),
        compiler_params=_cparams("parallel", "arbitrary"),
        name="matmul_residual",
    )(x, *ys, *ws)


def _moba_kernel(q_ref, k_ref, v_ref, o_ref, kb_ref, vt_ref, kmean_ref, sel_ref, *, seq, group):
    qi = pl.program_id(2)
    nb = seq // MOBA_BLOCK
    rows = MOBA_BLOCK
    grp = range(group)
    cols = [slice(j * HEAD_DIM, (j + 1) * HEAD_DIM) for j in grp]

    @pl.when(qi == 0)
    def _():
        for j in grp:
            for n in range(nb):
                kn = k_ref[n * rows:(n + 1) * rows, cols[j]]
                kb_ref[j, n] = kn.astype(kb_ref.dtype)
                kmean_ref[j, n:n + 1, :] = jnp.mean(kn, axis=0, keepdims=True)
                vt_ref[j, n] = v_ref[n * rows:(n + 1) * rows, cols[j]].T.astype(vt_ref.dtype)

    q = [q_ref[:, cols[j]] for j in grp]
    blk = lax.broadcasted_iota(jnp.int32, (nb, rows), 0)
    past = blk < qi
    gate = [jnp.where(past, lax.dot_general(kmean_ref[j], q[j], (((1,), (1,)), ((), ())), precision=HIGHEST,
                                            preferred_element_type=F32), NEG_INF) for j in grp]
    rank = [jnp.zeros((nb, rows), jnp.int32) for _ in grp]
    for m in range(nb):
        for j in grp:
            gm = gate[j][m:m + 1, :]
            beats = (gm > gate[j]) | ((gm == gate[j]) & (m < blk))
            rank[j] = rank[j] + beats.astype(jnp.int32)
    for j in grp:
        sel_ref[j] = jnp.where(past & (rank[j] < MOBA_TOPK), 1.0, 0.0)

    qb = [(q[j] * (HEAD_DIM ** -0.5)).astype(kb_ref.dtype) for j in grp]
    key = lax.broadcasted_iota(jnp.int32, (rows, rows), 0)
    qry = lax.broadcasted_iota(jnp.int32, (rows, rows), 1)
    s = [jnp.where(key <= qry, _mm_nt(kb_ref[j, qi], qb[j]), NEG_INF) for j in grp]
    m0 = [jnp.max(s[j], axis=0, keepdims=True) for j in grp]
    p = [jnp.exp(s[j] - m0[j]) for j in grp]
    l0 = [jnp.sum(p[j], axis=0, keepdims=True) for j in grp]
    acc0 = [_mm(vt_ref[j, qi], p[j]) for j in grp]

    def body(i, carry):
        m_run, l_run, acc = carry
        n0 = MOBA_STEP * i
        ss = [[jnp.where(sel_ref[j, pl.ds(n0 + d, 1), :] > 0.0, _mm_nt(kb_ref[j, n0 + d], qb[j]), NEG_INF)
               for d in range(MOBA_STEP)] for j in grp]
        m_new = list(m_run)
        for j in grp:
            for sj in ss[j]:
                m_new[j] = jnp.maximum(m_new[j], jnp.max(sj, axis=0, keepdims=True))
        alpha = [jnp.exp(m_run[j] - m_new[j]) for j in grp]
        ps = [[jnp.exp(sj - m_new[j]) for sj in ss[j]] for j in grp]
        l_new = [alpha[j] * l_run[j] for j in grp]
        acc = [alpha[j] * acc[j] for j in grp]
        for d in range(MOBA_STEP):
            for j in grp:
                l_new[j] = l_new[j] + jnp.sum(ps[j][d], axis=0, keepdims=True)
                acc[j] = acc[j] + _mm(vt_ref[j, n0 + d], ps[j][d])
        return tuple(m_new), tuple(l_new), tuple(acc)

    trips = (qi + MOBA_STEP - 1) // MOBA_STEP
    _, l_fin, acc = lax.fori_loop(0, trips, body, (tuple(m0), tuple(l0), tuple(acc0)))
    for j in grp:
        o_ref[:, cols[j]] = (acc[j] / l_fin[j]).T.astype(o_ref.dtype)


def _moba(proj, *, heads):
    b, t, _ = proj.shape
    nb = t // MOBA_BLOCK
    group = min(MOBA_GROUP, heads)
    assert t % MOBA_BLOCK == 0 and nb % MOBA_STEP == 0 and heads % group == 0
    width = group * HEAD_DIM
    ngrp = heads // group
    return pl.pallas_call(
        functools.partial(_moba_kernel, seq=t, group=group),
        grid=(b, ngrp, nb),
        in_specs=[pl.BlockSpec((None, MOBA_BLOCK, width), lambda bi, g, i: (bi, i, g)),
                  pl.BlockSpec((None, t, width), lambda bi, g, i: (bi, 0, ngrp + g)),
                  pl.BlockSpec((None, t, width), lambda bi, g, i: (bi, 0, 2 * ngrp + g))],
        out_specs=pl.BlockSpec((None, MOBA_BLOCK, width), lambda bi, g, i: (bi, i, g)),
        out_shape=jax.ShapeDtypeStruct((b, t, heads * HEAD_DIM), MXU_DTYPE),
        scratch_shapes=[pltpu.VMEM((group, nb, MOBA_BLOCK, HEAD_DIM), MXU_DTYPE),
                        pltpu.VMEM((group, nb, HEAD_DIM, MOBA_BLOCK), MXU_DTYPE),
                        pltpu.VMEM((group, nb, HEAD_DIM), F32),
                        pltpu.VMEM((group, nb, MOBA_BLOCK), F32)],
        compiler_params=_cparams("parallel", "parallel", "arbitrary"),
        name="moba",
    )(proj, proj, proj)


def _chunk_masks():
    row = lax.broadcasted_iota(jnp.int32, (CHUNK, CHUNK), 0)
    col = lax.broadcasted_iota(jnp.int32, (CHUNK, CHUNK), 1)
    return row >= col, row > col, (row == col).astype(F32)


def _conv_silu(prev, x, cw):
    xe = jnp.concatenate([prev, x], axis=0)
    acc = x * cw[CONV_W - 1:CONV_W, :]
    for k in range(1, CONV_W):
        acc = acc + pltpu.roll(xe, k, axis=0)[CONV_HALO:, :] * cw[CONV_W - 1 - k:CONV_W - k, :]
    return _silu(acc)


def _unit_lower_inverse(m_mats, eye):
    pows = [[-m] for m in m_mats]
    for _ in range(CHUNK.bit_length() - 2):
        for p in pows:
            p.append(_mm_x3(p[-1], p[-1]))
    terms = [[eye + pw for pw in p] for p in pows]
    while len(terms[0]) > 1:
        terms = [[_mm_x3(t[i], t[i + 1]) for i in range(0, len(t) - 1, 2)] + t[len(t) - len(t) % 2:]
                 for t in terms]
    return [t[0] for t in terms]


def _deltanet_kernel(q_ref, k_ref, v_ref, gate_ref, small_ref, cwq_ref, cwk_ref, cwv_ref,
                     alog_ref, dt_ref, ong_ref, o_ref, state_ref, halo_ref, *, heads, group, tile):
    hg = pl.program_id(1)
    tril, strict, eye = _chunk_masks()
    ones_tril = tril.astype(F32)
    lane = lax.broadcasted_iota(jnp.int32, (CHUNK, HEAD_DIM), 1)
    ong = ong_ref[...]

    @pl.when(pl.program_id(2) == 0)
    def _():
        state_ref[...] = jnp.zeros_like(state_ref)
        halo_ref[...] = jnp.zeros_like(halo_ref)

    grp = range(group)
    cols = [slice(j * HEAD_DIM, (j + 1) * HEAD_DIM) for j in grp]

    def body(c, carry):
        rows = pl.ds(pl.multiple_of(c * CHUNK, CHUNK), CHUNK)
        sm = small_ref[rows, :]
        dq, dk, dv, beta, g = [], [], [], [], []
        for j in grp:
            head = hg * group + j
            xq, xk, xv = q_ref[rows, cols[j]], k_ref[rows, cols[j]], v_ref[rows, cols[j]]
            cq = _conv_silu(halo_ref[3 * j], xq, cwq_ref[:, cols[j]])
            ck = _conv_silu(halo_ref[3 * j + 1], xk, cwk_ref[:, cols[j]])
            dv.append(_conv_silu(halo_ref[3 * j + 2], xv, cwv_ref[:, cols[j]]))
            halo_ref[3 * j] = xq[CHUNK - CONV_HALO:, :]
            halo_ref[3 * j + 1] = xk[CHUNK - CONV_HALO:, :]
            halo_ref[3 * j + 2] = xv[CHUNK - CONV_HALO:, :]
            dq.append(cq * lax.rsqrt(jnp.sum(cq * cq, axis=-1, keepdims=True) + EPS) * (HEAD_DIM ** -0.5))
            dk.append(ck * lax.rsqrt(jnp.sum(ck * ck, axis=-1, keepdims=True) + EPS))
            beta_raw = jnp.sum(jnp.where(lane == head, sm, 0.0), axis=1, keepdims=True)
            alpha_raw = jnp.sum(jnp.where(lane == heads + head, sm, 0.0), axis=1, keepdims=True)
            beta.append(_sigmoid(beta_raw))
            z = alpha_raw + dt_ref[pl.ds(head, 1), :]
            g.append(-jnp.exp(alog_ref[pl.ds(head, 1), :])
                     * (jnp.maximum(z, 0.0) + jnp.log1p(jnp.exp(-jnp.abs(z)))))
        gc = [_cumsum_rows(ones_tril, g[j]) for j in grp]
        gamma, e_g, kb = [], [], []
        for j in grp:
            g_col = gc[j][:, :CHUNK]
            g_row = jnp.sum(g_col * eye, axis=0, keepdims=True)
            gamma.append(jnp.exp(jnp.where(tril, g_col - g_row, NEG_INF)))
            e_g.append(jnp.exp(gc[j]))
            kb.append(dk[j] * beta[j])
        kq = [_mm_nt(jnp.concatenate([kb[j], dq[j]], axis=0), dk[j]) for j in grp]
        t_inv = _unit_lower_inverse([jnp.where(strict, kq[j][:CHUNK] * gamma[j], 0.0) for j in grp], eye)
        a_qk = [kq[j][CHUNK:] * gamma[j] for j in grp]
        uw = [_mm(t_inv[j], jnp.concatenate([dv[j] * beta[j], kb[j] * e_g[j]], axis=1)) for j in grp]
        state = [state_ref[j] for j in grp]
        ws = [_mm(jnp.concatenate([uw[j][:, HEAD_DIM:], dq[j] * e_g[j]], axis=0), state[j]) for j in grp]
        v_new = [uw[j][:, :HEAD_DIM] - ws[j][:CHUNK] for j in grp]
        o = [ws[j][CHUNK:] + _mm(a_qk[j], v_new[j]) for j in grp]
        for j in grp:
            g_last = gc[j][CHUNK - 1:CHUNK, :]
            state_ref[j] = jnp.exp(g_last) * state[j] + _mm_tn(dk[j] * jnp.exp(g_last - gc[j]), v_new[j])
        for j in grp:
            o_ref[rows, cols[j]] = (_rms(o[j], ong) * _silu(gate_ref[rows, cols[j]])).astype(o_ref.dtype)
        return carry

    lax.fori_loop(0, tile // CHUNK, body, 0)


def _deltanet(proj, small, conv_w, a_log, dt_bias, o_norm_g, *, col0, heads):
    b, t, _ = proj.shape
    group = min(DN_GROUP, heads)
    tile = min(SEQ_TILE, t)
    assert t % tile == 0 and tile % CHUNK == 0 and heads % group == 0 and col0 % group == 0
    width = group * HEAD_DIM

    def col(off):
        base = (col0 + off * heads) // group
        return pl.BlockSpec((None, tile, width), lambda bi, hg, ti: (bi, ti, base + hg))

    def cw(off):
        base = (off * heads) // group
        return pl.BlockSpec((CONV_W, width), lambda bi, hg, ti: (0, base + hg))

    rep = lambda a: jnp.broadcast_to(a.astype(F32)[:, None], (heads, HEAD_DIM))
    whole = lambda r: pl.BlockSpec((r, HEAD_DIM), lambda bi, hg, ti: (0, 0))
    return pl.pallas_call(
        functools.partial(_deltanet_kernel, heads=heads, group=group, tile=tile),
        grid=(b, heads // group, t // tile),
        in_specs=[col(0), col(1), col(2), col(3),
                  pl.BlockSpec((None, tile, HEAD_DIM), lambda bi, hg, ti: (bi, ti, 0)),
                  cw(0), cw(1), cw(2), whole(heads), whole(heads), whole(1)],
        out_specs=pl.BlockSpec((None, tile, width), lambda bi, hg, ti: (bi, ti, hg)),
        out_shape=jax.ShapeDtypeStruct((b, t, heads * HEAD_DIM), MXU_DTYPE),
        scratch_shapes=[pltpu.VMEM((group, HEAD_DIM, HEAD_DIM), F32),
                        pltpu.VMEM((3 * group, CONV_HALO, HEAD_DIM), F32)],
        compiler_params=_cparams("parallel", "parallel", "arbitrary"),
        name="deltanet",
    )(proj, proj, proj, proj, small, conv_w, conv_w, conv_w, rep(a_log), rep(dt_bias),
      o_norm_g.reshape(1, HEAD_DIM))


def _hgrn2_kernel(q_ref, f_ref, i_ref, g_ref, lbl_ref, ong_ref, o_ref, state_ref, *, layer, group, tile):
    tril, _, _ = _chunk_masks()
    ones_tril = tril.astype(F32)
    lbl = lbl_ref[...]
    e = jnp.exp(lbl - jnp.max(lbl, axis=0, keepdims=True))
    p = e / jnp.sum(e, axis=0, keepdims=True)
    cs = p[0:1, :]
    for r in range(1, layer + 1):
        cs = cs + p[r:r + 1, :]
    lb_all = cs - p[0:1, :]
    ong = ong_ref[...]
    pos = lax.broadcasted_iota(jnp.int32, (CHUNK, 1), 0) % HALF
    same_sub = (lax.broadcasted_iota(jnp.int32, (CHUNK // 2, CHUNK // 2), 0) // HALF
                == lax.broadcasted_iota(jnp.int32, (CHUNK // 2, CHUNK // 2), 1) // HALF)

    @pl.when(pl.program_id(2) == 0)
    def _():
        state_ref[...] = jnp.zeros_like(state_ref)

    grp = range(group)
    cols = [slice(j * HEAD_DIM, (j + 1) * HEAD_DIM) for j in grp]

    def body(c, carry):
        rows = pl.ds(pl.multiple_of(c * CHUNK, CHUNK), CHUNK)
        q, k, v, log_f = [], [], [], []
        for j in grp:
            lb = lb_all[:, cols[j]]
            fr = f_ref[rows, cols[j]]
            log_f.append(jnp.log(lb + (1.0 - lb) * _sigmoid(fr)))
            k.append((1.0 - lb) * _sigmoid(-fr))
            q.append(_silu(q_ref[rows, cols[j]]) * (HEAD_DIM ** -0.5))
            v.append(i_ref[rows, cols[j]])
        b = [_cumsum_rows(ones_tril, log_f[j]) for j in grp]
        state = [state_ref[j] for j in grp]
        o = [_mm_nt(q[j] * jnp.exp(b[j]), state[j]) for j in grp]
        scores = []
        for i in range(1, CHUNK // SUB):
            lo = i * SUB
            for j in grp:
                ref_row = b[j][lo:lo + 1, :]
                qt = q[j][lo:lo + SUB, :] * jnp.exp(b[j][lo:lo + SUB, :] - ref_row)
                kt = k[j][:lo, :] * jnp.exp(ref_row - b[j][:lo, :])
                scores.append(_mm_nt(qt, kt))
        below = [[jnp.zeros((SUB, HEAD_DIM), F32)] for _ in grp]
        for i in range(1, CHUNK // SUB):
            for j in grp:
                below[j].append(_mm(scores[(i - 1) * group + j], v[j][:i * SUB, :]))
        qh, kh, vh = [], [], []
        for j in grp:
            qt, kt, vt = [], [], []
            for i in range(CHUNK // SUB):
                lo, mid = i * SUB, i * SUB + HALF
                ref_row = b[j][mid:mid + 1, :]
                qt.append(q[j][mid:mid + HALF, :] * jnp.exp(b[j][mid:mid + HALF, :] - ref_row))
                kt.append(k[j][lo:mid, :] * jnp.exp(ref_row - b[j][lo:mid, :]))
                vt.append(v[j][lo:mid, :])
            qh.append(jnp.concatenate(qt, axis=0))
            kh.append(jnp.concatenate(kt, axis=0))
            vh.append(jnp.concatenate(vt, axis=0))
        half_scores = [jnp.where(same_sub, _mm_nt(qh[j], kh[j]), 0.0) for j in grp]
        half_out = [_mm(half_scores[j], vh[j]) for j in grp]
        zero_half = jnp.zeros((HALF, HEAD_DIM), F32)
        for j in grp:
            spread = []
            for i in range(CHUNK // SUB):
                spread += [zero_half, half_out[j][i * HALF:(i + 1) * HALF, :]]
            o[j] = o[j] + jnp.concatenate(below[j], axis=0) + jnp.concatenate(spread, axis=0)
        def rotate(a, d):
            blocks = a.reshape(CHUNK // HALF, HALF, HEAD_DIM)
            return pltpu.roll(blocks, d, axis=1).reshape(CHUNK, HEAD_DIM)

        for d in range(HALF):
            for j in grp:
                kr, br, vr = ((k[j], b[j], v[j]) if d == 0 else
                              tuple(rotate(a, d) for a in (k[j], b[j], v[j])))
                decay = jnp.exp(jnp.where(pos >= d, b[j] - br, NEG_INF))
                o[j] = o[j] + jnp.sum(q[j] * kr * decay, axis=1, keepdims=True) * vr
        for j in grp:
            b_last = b[j][CHUNK - 1:CHUNK, :]
            state_ref[j] = state[j] * jnp.exp(b_last) + _mm_tn(v[j], k[j] * jnp.exp(b_last - b[j]))
        for j in grp:
            o_ref[rows, cols[j]] = (_rms(o[j], ong) * _silu(g_ref[rows, cols[j]])).astype(o_ref.dtype)
        return carry

    lax.fori_loop(0, tile // CHUNK, body, 0)


def _hgrn2(proj, lb_logits, o_norm_g, *, layer, heads):
    b, t, _ = proj.shape
    depth = lb_logits.shape[0]
    group = min(HEAD_GROUP, heads)
    tile = min(SEQ_TILE, t)
    assert t % tile == 0 and tile % CHUNK == 0 and heads % group == 0
    width = group * HEAD_DIM

    def col(off):
        base = (off * heads) // group
        return pl.BlockSpec((None, tile, width), lambda bi, hg, ti: (bi, ti, base + hg))

    return pl.pallas_call(
        functools.partial(_hgrn2_kernel, layer=layer, group=group, tile=tile),
        grid=(b, heads // group, t // tile),
        in_specs=[col(0), col(1), col(2), col(3),
                  pl.BlockSpec((depth, width), lambda bi, hg, ti: (0, hg)),
                  pl.BlockSpec((1, HEAD_DIM), lambda bi, hg, ti: (0, 0))],
        out_specs=pl.BlockSpec((None, tile, width), lambda bi, hg, ti: (bi, ti, hg)),
        out_shape=jax.ShapeDtypeStruct((b, t, heads * HEAD_DIM), MXU_DTYPE),
        scratch_shapes=[pltpu.VMEM((group, HEAD_DIM, HEAD_DIM), F32)],
        compiler_params=_cparams("parallel", "parallel", "arbitrary"),
        name="hgrn2",
    )(proj, proj, proj, proj, lb_logits.astype(F32), o_norm_g.reshape(1, HEAD_DIM))


def _cross_attn_kernel(x_ref, g_ref, wq_ref, kv_ref, wo_ref, o_ref):
    x = x_ref[...]
    xn = _rms(x, g_ref[...])
    q = _mm(xn, wq_ref[...])
    width = X_HEADS * HEAD_DIM
    scale = HEAD_DIM ** -0.5
    outs = []
    for h in range(X_HEADS):
        lo = h * HEAD_DIM
        s = _mm_nt(q[:, lo:lo + HEAD_DIM], kv_ref[:, lo:lo + HEAD_DIM]) * scale
        p = jnp.exp(s - jnp.max(s, axis=-1, keepdims=True))
        p = p / jnp.sum(p, axis=-1, keepdims=True)
        outs.append(_mm(p, kv_ref[:, width + lo:width + lo + HEAD_DIM]))
    o_ref[...] = x + _mm(jnp.concatenate(outs, axis=-1), wo_ref[...])


def _cross_attn(x, g, wq, kv, wo, *, layer, seq, tm):
    m, d = x.shape
    mem_tokens = kv.shape[0] // (m // seq)
    tm = min(tm, seq)
    assert seq % tm == 0
    per_seq = seq // tm
    width = X_HEADS * HEAD_DIM
    return pl.pallas_call(
        _cross_attn_kernel,
        grid=(m // tm,),
        in_specs=[pl.BlockSpec((tm, d), lambda i: (i, 0)),
                  pl.BlockSpec((1, d), lambda i: (0, 0)),
                  pl.BlockSpec((d, width), lambda i: (0, 0)),
                  pl.BlockSpec((mem_tokens, 2 * width), lambda i: (i // per_seq, layer)),
                  pl.BlockSpec((width, d), lambda i: (0, 0))],
        out_specs=pl.BlockSpec((tm, d), lambda i: (i, 0)),
        out_shape=jax.ShapeDtypeStruct((m, d), F32),
        compiler_params=_cparams("parallel"),
        name="cross_attn",
    )(x, g.reshape(1, d), wq, kv, wo)


TM = 512
TM_PROJ = 1024
TN = 1024
TN_OUT = 2048
TF = 512


def _moba_deltanet_mixer(x, g, w_in, conv_w, a_log, dt_bias, o_norm_g, w_out, *, batch):
    m, d = x.shape
    seq = m // batch
    dn_heads = a_log.shape[0]
    main = w_in.shape[1] - 2 * dn_heads
    dn_width = dn_heads * HEAD_DIM
    moba_heads = (main - 4 * dn_width) // (3 * HEAD_DIM)
    proj = _norm_matmul(x, g, w_in[:, :main].astype(MXU_DTYPE), tm=TM_PROJ, tn=TN)
    w_small = jnp.pad(w_in[:, main:], ((0, 0), (0, HEAD_DIM - 2 * dn_heads))).astype(MXU_DTYPE)
    small = _norm_matmul(x, g, w_small, tm=TM, tn=HEAD_DIM)
    proj = proj.reshape(batch, seq, main)
    small = small.reshape(batch, seq, HEAD_DIM)
    y_a = _moba(proj, heads=moba_heads)
    y_b = _deltanet(proj, small, conv_w, a_log, dt_bias, o_norm_g, col0=3 * moba_heads, heads=dn_heads)
    wa = w_out[:moba_heads * HEAD_DIM].astype(MXU_DTYPE)
    wb = w_out[moba_heads * HEAD_DIM:].astype(MXU_DTYPE)
    return _matmul_residual(x, [y_a.reshape(m, -1), y_b.reshape(m, -1)], [wa, wb], tm=TM, tn=TN_OUT)


def _hgrn2_mixer(x, g, w_in, lb_logits, o_norm_g, w_out, *, batch, layer):
    m, d = x.shape
    seq = m // batch
    width = w_in.shape[1] // 4
    proj = _norm_matmul(x, g, w_in.astype(MXU_DTYPE), tm=TM_PROJ, tn=TN).reshape(batch, seq, 4 * width)
    y = _hgrn2(proj, lb_logits, o_norm_g, layer=layer, heads=width // HEAD_DIM)
    return _matmul_residual(x, [y.reshape(m, width)], [w_out.astype(MXU_DTYPE)], tm=TM, tn=TN_OUT)


def kernel(x, mem, norm_g, mem_norm_g, final_norm_g, ffn_w_in, ffn_w_out, ab_w_in, ab_conv_w, ab_a_log, ab_dt_bias, ab_o_norm_g, ab_w_out, c_w_in, c_lb_logits, c_o_norm_g, c_w_out, x_w_q, x_w_kv, x_w_o):
    batch, seq, d = x.shape
    depth = norm_g.shape[0]
    m = batch * seq
    xf = x.reshape(m, d).astype(F32)
    w_kv = jnp.transpose(x_w_kv, (1, 0, 2)).reshape(d, -1).astype(MXU_DTYPE)
    kv = _norm_matmul(mem.reshape(-1, d).astype(F32), mem_norm_g, w_kv, tm=TM, tn=TN)
    ffn_weights = _ffn_prep(ffn_w_in, ffn_w_out, TF)
    for l in range(depth):
        xf = _ffn(xf, norm_g[l, 0], ffn_weights, layer=l, slot=0, tm=TM, tf=TF)
        if l % 2 == 0:
            e = l // 2
            xf = _moba_deltanet_mixer(xf, norm_g[l, 1], ab_w_in[e], ab_conv_w[e], ab_a_log[e], ab_dt_bias[e],
                                      ab_o_norm_g[e], ab_w_out[e], batch=batch)
        else:
            o = l // 2
            xf = _hgrn2_mixer(xf, norm_g[l, 1], c_w_in[o], c_lb_logits, c_o_norm_g[o], c_w_out[o],
                              batch=batch, layer=l)
        xf = _cross_attn(xf, norm_g[l, 2], x_w_q[l].astype(MXU_DTYPE), kv, x_w_o[l].astype(MXU_DTYPE),
                         layer=l, seq=seq, tm=TM)
        xf = _ffn(xf, norm_g[l, 3], ffn_weights, layer=l, slot=1, tm=TM, tf=TF,
                  final_g=final_norm_g if l == depth - 1 else None)
    return xf.reshape(batch, seq, d)
```

```python
import functools

import jax
import jax.numpy as jnp
from jax import lax
from jax.experimental import pallas as pl
from jax.experimental.pallas import tpu as pltpu

F32 = jnp.float32
MXU_DTYPE = jnp.bfloat16
HIGHEST = lax.Precision.HIGHEST
EPS = 1e-6
NEG_INF = float("-inf")

HEAD_DIM = 128
MOBA_BLOCK = 256
MOBA_TOPK = 3
MOBA_STEP = 2
CHUNK = 64
SUB = 16
HALF = SUB // 2
CONV_W = 4
CONV_HALO = 8
X_HEADS = 4
MOBA_GROUP = 4
DN_GROUP = 8
HEAD_GROUP = 8
SEQ_TILE = 512
VMEM_LIMIT = 56 * 1024 * 1024


def _cparams(*sem):
    return pltpu.CompilerParams(dimension_semantics=sem, vmem_limit_bytes=VMEM_LIMIT)


def _tile(n, t):
    if n <= t:
        return n
    t -= t % HEAD_DIM
    while n % t:
        t -= HEAD_DIM
    return t


def _mm(a, b):
    return jnp.dot(a.astype(MXU_DTYPE), b.astype(MXU_DTYPE), preferred_element_type=F32)


def _mm_nt(a, b):
    return lax.dot_general(a.astype(MXU_DTYPE), b.astype(MXU_DTYPE), (((1,), (1,)), ((), ())),
                           preferred_element_type=F32)


def _mm_tn(a, b):
    return lax.dot_general(a.astype(MXU_DTYPE), b.astype(MXU_DTYPE), (((0,), (0,)), ((), ())),
                           preferred_element_type=F32)


def _split2(x):
    hi = x.astype(MXU_DTYPE)
    return hi, (x - hi.astype(F32)).astype(MXU_DTYPE)


def _mm_x3(a, b):
    ah, al = _split2(a)
    bh, bl = _split2(b)
    dot = functools.partial(jnp.dot, preferred_element_type=F32)
    return dot(ah, bh) + (dot(ah, bl) + dot(al, bh))


def _cumsum_rows(ones_tril, x):
    hi = x.astype(MXU_DTYPE)
    rest = x - hi.astype(F32)
    mid = rest.astype(MXU_DTYPE)
    lo = (rest - mid.astype(F32)).astype(MXU_DTYPE)
    dot = functools.partial(jnp.dot, ones_tril.astype(MXU_DTYPE), preferred_element_type=F32)
    return dot(hi) + (dot(mid) + dot(lo))


def _sigmoid(x):
    return 1.0 / (1.0 + jnp.exp(-x))


def _silu(x):
    return x * _sigmoid(x)


def _rms(x, g):
    return x * lax.rsqrt(jnp.mean(x * x, axis=-1, keepdims=True) + EPS) * g


def _norm_matmul_kernel(x_ref, g_ref, w_ref, o_ref, xn_ref):
    @pl.when(pl.program_id(1) == 0)
    def _():
        xn_ref[...] = _rms(x_ref[...], g_ref[...]).astype(xn_ref.dtype)

    o_ref[...] = jnp.dot(xn_ref[...], w_ref[...], preferred_element_type=F32)


def _norm_matmul(x, g, w, *, tm, tn):
    m, k = x.shape
    n = w.shape[1]
    tm, tn = _tile(m, tm), _tile(n, tn)
    return pl.pallas_call(
        _norm_matmul_kernel,
        grid=(m // tm, n // tn),
        in_specs=[pl.BlockSpec((tm, k), lambda i, j: (i, 0)),
                  pl.BlockSpec((1, k), lambda i, j: (0, 0)),
                  pl.BlockSpec((k, tn), lambda i, j: (0, j))],
        out_specs=pl.BlockSpec((tm, tn), lambda i, j: (i, j)),
        out_shape=jax.ShapeDtypeStruct((m, n), F32),
        scratch_shapes=[pltpu.VMEM((tm, k), MXU_DTYPE)],
        compiler_params=_cparams("parallel", "arbitrary"),
        name="norm_matmul",
    )(x, g.reshape(1, k), w)


def _ffn_kernel(*refs, has_tail, has_final):
    x_ref, g_ref, wa_ref, wb_ref, wo_ref = refs[:5]
    rest = list(refs[5:])
    tail_refs = [rest.pop(0) for _ in range(3)] if has_tail else None
    final_g_ref = rest.pop(0) if has_final else None
    o_ref, xn_ref, acc_ref = rest
    j = pl.program_id(1)

    @pl.when(j == 0)
    def _():
        xn_ref[...] = _rms(x_ref[...], g_ref[...]).astype(xn_ref.dtype)
        acc_ref[...] = jnp.zeros_like(acc_ref)

    def swiglu(wa, wb, wo):
        xn = xn_ref[...]
        a = jnp.dot(xn, wa[...], preferred_element_type=F32)
        b = jnp.dot(xn, wb[...], preferred_element_type=F32)
        return _mm(_silu(a) * b, wo[...])

    acc_ref[...] += swiglu(wa_ref, wb_ref, wo_ref)

    @pl.when(j == pl.num_programs(1) - 1)
    def _():
        acc = acc_ref[...]
        if has_tail:
            acc = acc + swiglu(*tail_refs)
        y = x_ref[...] + 0.5 * acc
        if has_final:
            y = _rms(y, final_g_ref[...])
        o_ref[...] = y


def _ffn(x, g, weights, *, layer, slot, tm, tf, final_g=None):
    w_a, w_b, w_out, tails = weights
    m, d = x.shape
    f = w_out.shape[2]
    tm = min(tm, m)
    assert m % tm == 0 and f >= tf
    pick = lambda r, c: pl.BlockSpec((None, None, r, c), lambda i, j: (layer, slot, 0, 0))
    in_specs = [pl.BlockSpec((tm, d), lambda i, j: (i, 0)),
                pl.BlockSpec((1, d), lambda i, j: (0, 0)),
                pl.BlockSpec((None, None, d, tf), lambda i, j: (layer, slot, 0, j)),
                pl.BlockSpec((None, None, d, tf), lambda i, j: (layer, slot, 0, j)),
                pl.BlockSpec((None, None, tf, d), lambda i, j: (layer, slot, j, 0))]
    args = [x, g.reshape(1, d), w_a, w_b, w_out]
    if tails is not None:
        rem = f % tf
        in_specs += [pick(d, rem), pick(d, rem), pick(rem, d)]
        args += list(tails)
    if final_g is not None:
        in_specs.append(pl.BlockSpec((1, d), lambda i, j: (0, 0)))
        args.append(final_g.reshape(1, d))
    return pl.pallas_call(
        functools.partial(_ffn_kernel, has_tail=tails is not None, has_final=final_g is not None),
        grid=(m // tm, f // tf),
        in_specs=in_specs,
        out_specs=pl.BlockSpec((tm, d), lambda i, j: (i, 0)),
        out_shape=jax.ShapeDtypeStruct((m, d), F32),
        scratch_shapes=[pltpu.VMEM((tm, d), MXU_DTYPE), pltpu.VMEM((tm, d), F32)],
        compiler_params=_cparams("parallel", "arbitrary"),
        name="ffn",
    )(*args)


def _ffn_prep(ffn_w_in, ffn_w_out, tf):
    f = ffn_w_out.shape[2]
    w_a = ffn_w_in.astype(MXU_DTYPE)
    w_b = w_a[..., f:]
    w_out = ffn_w_out.astype(MXU_DTYPE)
    main = (f // tf) * tf
    tails = None if main == f else (w_a[..., main:f], w_b[..., main:], w_out[:, :, main:, :])
    return w_a, w_b, w_out, tails


def _matmul_residual_kernel(*refs, n_in):
    x_ref, o_ref = refs[0], refs[-1]
    acc = x_ref[...]
    for i in range(n_in):
        acc = acc + _mm(refs[1 + i][...], refs[1 + n_in + i][...])
    o_ref[...] = acc


def _matmul_residual(x, ys, ws, *, tm, tn):
    m, n = x.shape
    tm, tn = _tile(m, tm), _tile(n, tn)
    n_in = len(ys)
    in_specs = [pl.BlockSpec((tm, tn), lambda i, j: (i, j))]
    in_specs += [pl.BlockSpec((tm, y.shape[1]), lambda i, j: (i, 0)) for y in ys]
    in_specs += [pl.BlockSpec((w.shape[0], tn), lambda i, j: (0, j)) for w in ws]
    return pl.pallas_call(
        functools.partial(_matmul_residual_kernel, n_in=n_in),
        grid=(m // tm, n // tn),
        in_specs=in_specs,
        out_specs=pl.BlockSpec((tm, tn), lambda i, j: (i, j)),
        out_shape=jax.ShapeDtypeStruct((m, n), F32),
        compiler_params=_cparams("parallel", "arbitrary"),
        name="matmul_residual",
    )(x, *ys, *ws)


def _moba_kernel(q_ref, k_ref, v_ref, o_ref, kb_ref, vt_ref, kmean_ref, sel_ref, *, seq, group):
    qi = pl.program_id(2)
    nb = seq // MOBA_BLOCK
    rows = MOBA_BLOCK
    grp = range(group)
    cols = [slice(j * HEAD_DIM, (j + 1) * HEAD_DIM) for j in grp]

    @pl.when(qi == 0)
    def _():
        for j in grp:
            for n in range(nb):
                kn = k_ref[n * rows:(n + 1) * rows, cols[j]]
                kb_ref[j, n] = kn.astype(kb_ref.dtype)
                kmean_ref[j, n:n + 1, :] = jnp.mean(kn, axis=0, keepdims=True)
                vt_ref[j, n] = v_ref[n * rows:(n + 1) * rows, cols[j]].T.astype(vt_ref.dtype)

    q = [q_ref[:, cols[j]] for j in grp]
    blk = lax.broadcasted_iota(jnp.int32, (nb, rows), 0)
    past = blk < qi
    gate = [jnp.where(past, lax.dot_general(kmean_ref[j], q[j], (((1,), (1,)), ((), ())), precision=HIGHEST,
                                            preferred_element_type=F32), NEG_INF) for j in grp]
    rank = [jnp.zeros((nb, rows), jnp.int32) for _ in grp]
    for m in range(nb):
        for j in grp:
            gm = gate[j][m:m + 1, :]
            beats = (gm > gate[j]) | ((gm == gate[j]) & (m < blk))
            rank[j] = rank[j] + beats.astype(jnp.int32)
    for j in grp:
        sel_ref[j] = jnp.where(past & (rank[j] < MOBA_TOPK), 1.0, 0.0)

    qb = [(q[j] * (HEAD_DIM ** -0.5)).astype(kb_ref.dtype) for j in grp]
    key = lax.broadcasted_iota(jnp.int32, (rows, rows), 0)
    qry = lax.broadcasted_iota(jnp.int32, (rows, rows), 1)
    s = [jnp.where(key <= qry, _mm_nt(kb_ref[j, qi], qb[j]), NEG_INF) for j in grp]
    m0 = [jnp.max(s[j], axis=0, keepdims=True) for j in grp]
    p = [jnp.exp(s[j] - m0[j]) for j in grp]
    l0 = [jnp.sum(p[j], axis=0, keepdims=True) for j in grp]
    acc0 = [_mm(vt_ref[j, qi], p[j]) for j in grp]

    def body(i, carry):
        m_run, l_run, acc = carry
        n0 = MOBA_STEP * i
        ss = [[jnp.where(sel_ref[j, pl.ds(n0 + d, 1), :] > 0.0, _mm_nt(kb_ref[j, n0 + d], qb[j]), NEG_INF)
               for d in range(MOBA_STEP)] for j in grp]
        m_new = list(m_run)
        for j in grp:
            for sj in ss[j]:
                m_new[j] = jnp.maximum(m_new[j], jnp.max(sj, axis=0, keepdims=True))
        alpha = [jnp.exp(m_run[j] - m_new[j]) for j in grp]
        ps = [[jnp.exp(sj - m_new[j]) for sj in ss[j]] for j in grp]
        l_new = [alpha[j] * l_run[j] for j in grp]
        acc = [alpha[j] * acc[j] for j in grp]
        for d in range(MOBA_STEP):
            for j in grp:
                l_new[j] = l_new[j] + jnp.sum(ps[j][d], axis=0, keepdims=True)
                acc[j] = acc[j] + _mm(vt_ref[j, n0 + d], ps[j][d])
        return tuple(m_new), tuple(l_new), tuple(acc)

    trips = (qi + MOBA_STEP - 1) // MOBA_STEP
    _, l_fin, acc = lax.fori_loop(0, trips, body, (tuple(m0), tuple(l0), tuple(acc0)))
    for j in grp:
        o_ref[:, cols[j]] = (acc[j] / l_fin[j]).T.astype(o_ref.dtype)


def _moba(proj, *, heads):
    b, t, _ = proj.shape
    nb = t // MOBA_BLOCK
    group = min(MOBA_GROUP, heads)
    assert t % MOBA_BLOCK == 0 and nb % MOBA_STEP == 0 and heads % group == 0
    width = group * HEAD_DIM
    ngrp = heads // group
    return pl.pallas_call(
        functools.partial(_moba_kernel, seq=t, group=group),
        grid=(b, ngrp, nb),
        in_specs=[pl.BlockSpec((None, MOBA_BLOCK, width), lambda bi, g, i: (bi, i, g)),
                  pl.BlockSpec((None, t, width), lambda bi, g, i: (bi, 0, ngrp + g)),
                  pl.BlockSpec((None, t, width), lambda bi, g, i: (bi, 0, 2 * ngrp + g))],
        out_specs=pl.BlockSpec((None, MOBA_BLOCK, width), lambda bi, g, i: (bi, i, g)),
        out_shape=jax.ShapeDtypeStruct((b, t, heads * HEAD_DIM), MXU_DTYPE),
        scratch_shapes=[pltpu.VMEM((group, nb, MOBA_BLOCK, HEAD_DIM), MXU_DTYPE),
                        pltpu.VMEM((group, nb, HEAD_DIM, MOBA_BLOCK), MXU_DTYPE),
                        pltpu.VMEM((group, nb, HEAD_DIM), F32),
                        pltpu.VMEM((group, nb, MOBA_BLOCK), F32)],
        compiler_params=_cparams("parallel", "parallel", "arbitrary"),
        name="moba",
    )(proj, proj, proj)


def _chunk_masks():
    row = lax.broadcasted_iota(jnp.int32, (CHUNK, CHUNK), 0)
    col = lax.broadcasted_iota(jnp.int32, (CHUNK, CHUNK), 1)
    return row >= col, row > col, (row == col).astype(F32)


def _conv_silu(prev, x, cw):
    xe = jnp.concatenate([prev, x], axis=0)
    acc = x * cw[CONV_W - 1:CONV_W, :]
    for k in range(1, CONV_W):
        acc = acc + pltpu.roll(xe, k, axis=0)[CONV_HALO:, :] * cw[CONV_W - 1 - k:CONV_W - k, :]
    return _silu(acc)


def _unit_lower_inverse(m_mats, eye):
    pows = [[-m] for m in m_mats]
    for _ in range(CHUNK.bit_length() - 2):
        for p in pows:
            p.append(_mm_x3(p[-1], p[-1]))
    terms = [[eye + pw for pw in p] for p in pows]
    while len(terms[0]) > 1:
        terms = [[_mm_x3(t[i], t[i + 1]) for i in range(0, len(t) - 1, 2)] + t[len(t) - len(t) % 2:]
                 for t in terms]
    return [t[0] for t in terms]


def _deltanet_kernel(q_ref, k_ref, v_ref, gate_ref, small_ref, cwq_ref, cwk_ref, cwv_ref,
                     alog_ref, dt_ref, ong_ref, o_ref, state_ref, halo_ref, *, heads, group, tile):
    hg = pl.program_id(1)
    tril, strict, eye = _chunk_masks()
    ones_tril = tril.astype(F32)
    lane = lax.broadcasted_iota(jnp.int32, (CHUNK, HEAD_DIM), 1)
    ong = ong_ref[...]

    @pl.when(pl.program_id(2) == 0)
    def _():
        state_ref[...] = jnp.zeros_like(state_ref)
        halo_ref[...] = jnp.zeros_like(halo_ref)

    grp = range(group)
    cols = [slice(j * HEAD_DIM, (j + 1) * HEAD_DIM) for j in grp]

    def body(c, carry):
        rows = pl.ds(pl.multiple_of(c * CHUNK, CHUNK), CHUNK)
        sm = small_ref[rows, :]
        dq, dk, dv, beta, g = [], [], [], [], []
        for j in grp:
            head = hg * group + j
            xq, xk, xv = q_ref[rows, cols[j]], k_ref[rows, cols[j]], v_ref[rows, cols[j]]
            cq = _conv_silu(halo_ref[3 * j], xq, cwq_ref[:, cols[j]])
            ck = _conv_silu(halo_ref[3 * j + 1], xk, cwk_ref[:, cols[j]])
            dv.append(_conv_silu(halo_ref[3 * j + 2], xv, cwv_ref[:, cols[j]]))
            halo_ref[3 * j] = xq[CHUNK - CONV_HALO:, :]
            halo_ref[3 * j + 1] = xk[CHUNK - CONV_HALO:, :]
            halo_ref[3 * j + 2] = xv[CHUNK - CONV_HALO:, :]
            dq.append(cq * lax.rsqrt(jnp.sum(cq * cq, axis=-1, keepdims=True) + EPS) * (HEAD_DIM ** -0.5))
            dk.append(ck * lax.rsqrt(jnp.sum(ck * ck, axis=-1, keepdims=True) + EPS))
            beta_raw = jnp.sum(jnp.where(lane == head, sm, 0.0), axis=1, keepdims=True)
            alpha_raw = jnp.sum(jnp.where(lane == heads + head, sm, 0.0), axis=1, keepdims=True)
            beta.append(_sigmoid(beta_raw))
            z = alpha_raw + dt_ref[pl.ds(head, 1), :]
            g.append(-jnp.exp(alog_ref[pl.ds(head, 1), :])
                     * (jnp.maximum(z, 0.0) + jnp.log1p(jnp.exp(-jnp.abs(z)))))
        gc = [_cumsum_rows(ones_tril, g[j]) for j in grp]
        gamma, e_g, kb = [], [], []
        for j in grp:
            g_col = gc[j][:, :CHUNK]
            g_row = jnp.sum(g_col * eye, axis=0, keepdims=True)
            gamma.append(jnp.exp(jnp.where(tril, g_col - g_row, NEG_INF)))
            e_g.append(jnp.exp(gc[j]))
            kb.append(dk[j] * beta[j])
        kq = [_mm_nt(jnp.concatenate([kb[j], dq[j]], axis=0), dk[j]) for j in grp]
        t_inv = _unit_lower_inverse([jnp.where(strict, kq[j][:CHUNK] * gamma[j], 0.0) for j in grp], eye)
        a_qk = [kq[j][CHUNK:] * gamma[j] for j in grp]
        uw = [_mm(t_inv[j], jnp.concatenate([dv[j] * beta[j], kb[j] * e_g[j]], axis=1)) for j in grp]
        state = [state_ref[j] for j in grp]
        ws = [_mm(jnp.concatenate([uw[j][:, HEAD_DIM:], dq[j] * e_g[j]], axis=0), state[j]) for j in grp]
        v_new = [uw[j][:, :HEAD_DIM] - ws[j][:CHUNK] for j in grp]
        o = [ws[j][CHUNK:] + _mm(a_qk[j], v_new[j]) for j in grp]
        for j in grp:
            g_last = gc[j][CHUNK - 1:CHUNK, :]
            state_ref[j] = jnp.exp(g_last) * state[j] + _mm_tn(dk[j] * jnp.exp(g_last - gc[j]), v_new[j])
        for j in grp:
            o_ref[rows, cols[j]] = (_rms(o[j], ong) * _silu(gate_ref[rows, cols[j]])).astype(o_ref.dtype)
        return carry

    lax.fori_loop(0, tile // CHUNK, body, 0)


def _deltanet(proj, small, conv_w, a_log, dt_bias, o_norm_g, *, col0, heads):
    b, t, _ = proj.shape
    group = min(DN_GROUP, heads)
    tile = min(SEQ_TILE, t)
    assert t % tile == 0 and tile % CHUNK == 0 and heads % group == 0 and col0 % group == 0
    width = group * HEAD_DIM

    def col(off):
        base = (col0 + off * heads) // group
        return pl.BlockSpec((None, tile, width), lambda bi, hg, ti: (bi, ti, base + hg))

    def cw(off):
        base = (off * heads) // group
        return pl.BlockSpec((CONV_W, width), lambda bi, hg, ti: (0, base + hg))

    rep = lambda a: jnp.broadcast_to(a.astype(F32)[:, None], (heads, HEAD_DIM))
    whole = lambda r: pl.BlockSpec((r, HEAD_DIM), lambda bi, hg, ti: (0, 0))
    return pl.pallas_call(
        functools.partial(_deltanet_kernel, heads=heads, group=group, tile=tile),
        grid=(b, heads // group, t // tile),
        in_specs=[col(0), col(1), col(2), col(3),
                  pl.BlockSpec((None, tile, HEAD_DIM), lambda bi, hg, ti: (bi, ti, 0)),
                  cw(0), cw(1), cw(2), whole(heads), whole(heads), whole(1)],
        out_specs=pl.BlockSpec((None, tile, width), lambda bi, hg, ti: (bi, ti, hg)),
        out_shape=jax.ShapeDtypeStruct((b, t, heads * HEAD_DIM), MXU_DTYPE),
        scratch_shapes=[pltpu.VMEM((group, HEAD_DIM, HEAD_DIM), F32),
                        pltpu.VMEM((3 * group, CONV_HALO, HEAD_DIM), F32)],
        compiler_params=_cparams("parallel", "parallel", "arbitrary"),
        name="deltanet",
    )(proj, proj, proj, proj, small, conv_w, conv_w, conv_w, rep(a_log), rep(dt_bias),
      o_norm_g.reshape(1, HEAD_DIM))


def _hgrn2_kernel(q_ref, f_ref, i_ref, g_ref, lbl_ref, ong_ref, o_ref, state_ref, *, layer, group, tile):
    tril, _, _ = _chunk_masks()
    ones_tril = tril.astype(F32)
    lbl = lbl_ref[...]
    e = jnp.exp(lbl - jnp.max(lbl, axis=0, keepdims=True))
    p = e / jnp.sum(e, axis=0, keepdims=True)
    cs = p[0:1, :]
    for r in range(1, layer + 1):
        cs = cs + p[r:r + 1, :]
    lb_all = cs - p[0:1, :]
    ong = ong_ref[...]
    pos = lax.broadcasted_iota(jnp.int32, (CHUNK, 1), 0) % HALF
    same_sub = (lax.broadcasted_iota(jnp.int32, (CHUNK // 2, CHUNK // 2), 0) // HALF
                == lax.broadcasted_iota(jnp.int32, (CHUNK // 2, CHUNK // 2), 1) // HALF)

    @pl.when(pl.program_id(2) == 0)
    def _():
        state_ref[...] = jnp.zeros_like(state_ref)

    grp = range(group)
    cols = [slice(j * HEAD_DIM, (j + 1) * HEAD_DIM) for j in grp]

    def body(c, carry):
        rows = pl.ds(pl.multiple_of(c * CHUNK, CHUNK), CHUNK)
        q, k, v, log_f, f_in = [], [], [], [], []
        for j in grp:
            lb = lb_all[:, cols[j]]
            fr = f_ref[rows, cols[j]]
            f = lb + (1.0 - lb) * _sigmoid(fr)
            f_in.append(jnp.where(pos == 0, 0.0, f))
            log_f.append(jnp.log(f))
            k.append((1.0 - lb) * _sigmoid(-fr))
            q.append(_silu(q_ref[rows, cols[j]]) * (HEAD_DIM ** -0.5))
            v.append(i_ref[rows, cols[j]])
        b = [_cumsum_rows(ones_tril, log_f[j]) for j in grp]
        state = [state_ref[j] for j in grp]
        o = [_mm_nt(q[j] * jnp.exp(b[j]), state[j]) for j in grp]
        scores = []
        for i in range(1, CHUNK // SUB):
            lo = i * SUB
            for j in grp:
                ref_row = b[j][lo:lo + 1, :]
                qt = q[j][lo:lo + SUB, :] * jnp.exp(b[j][lo:lo + SUB, :] - ref_row)
                kt = k[j][:lo, :] * jnp.exp(ref_row - b[j][:lo, :])
                scores.append(_mm_nt(qt, kt))
        below = [[jnp.zeros((SUB, HEAD_DIM), F32)] for _ in grp]
        for i in range(1, CHUNK // SUB):
            for j in grp:
                below[j].append(_mm(scores[(i - 1) * group + j], v[j][:i * SUB, :]))
        qh, kh, vh = [], [], []
        for j in grp:
            qt, kt, vt = [], [], []
            for i in range(CHUNK // SUB):
                lo, mid = i * SUB, i * SUB + HALF
                ref_row = b[j][mid:mid + 1, :]
                qt.append(q[j][mid:mid + HALF, :] * jnp.exp(b[j][mid:mid + HALF, :] - ref_row))
                kt.append(k[j][lo:mid, :] * jnp.exp(ref_row - b[j][lo:mid, :]))
                vt.append(v[j][lo:mid, :])
            qh.append(jnp.concatenate(qt, axis=0))
            kh.append(jnp.concatenate(kt, axis=0))
            vh.append(jnp.concatenate(vt, axis=0))
        half_scores = [jnp.where(same_sub, _mm_nt(qh[j], kh[j]), 0.0) for j in grp]
        half_out = [_mm(half_scores[j], vh[j]) for j in grp]
        zero_half = jnp.zeros((HALF, HEAD_DIM), F32)
        for j in grp:
            spread = []
            for i in range(CHUNK // SUB):
                spread += [zero_half, half_out[j][i * HALF:(i + 1) * HALF, :]]
            o[j] = o[j] + jnp.concatenate(below[j], axis=0) + jnp.concatenate(spread, axis=0)
        def shift1(a):
            blocks = a.reshape(CHUNK // HALF, HALF, HEAD_DIM)
            return pltpu.roll(blocks, 1, axis=1).reshape(CHUNK, HEAD_DIM)

        kd, vd = list(k), list(v)
        for d in range(HALF):
            for j in grp:
                if d:
                    kd[j] = f_in[j] * shift1(kd[j])
                    vd[j] = shift1(vd[j])
                o[j] = o[j] + jnp.sum(q[j] * kd[j], axis=1, keepdims=True) * vd[j]
        for j in grp:
            b_last = b[j][CHUNK - 1:CHUNK, :]
            state_ref[j] = state[j] * jnp.exp(b_last) + _mm_tn(v[j], k[j] * jnp.exp(b_last - b[j]))
        for j in grp:
            o_ref[rows, cols[j]] = (_rms(o[j], ong) * _silu(g_ref[rows, cols[j]])).astype(o_ref.dtype)
        return carry

    lax.fori_loop(0, tile // CHUNK, body, 0)


def _hgrn2(proj, lb_logits, o_norm_g, *, layer, heads):
    b, t, _ = proj.shape
    depth = lb_logits.shape[0]
    group = min(HEAD_GROUP, heads)
    tile = min(SEQ_TILE, t)
    assert t % tile == 0 and tile % CHUNK == 0 and heads % group == 0
    width = group * HEAD_DIM

    def col(off):
        base = (off * heads) // group
        return pl.BlockSpec((None, tile, width), lambda bi, hg, ti: (bi, ti, base + hg))

    return pl.pallas_call(
        functools.partial(_hgrn2_kernel, layer=layer, group=group, tile=tile),
        grid=(b, heads // group, t // tile),
        in_specs=[col(0), col(1), col(2), col(3),
                  pl.BlockSpec((depth, width), lambda bi, hg, ti: (0, hg)),
                  pl.BlockSpec((1, HEAD_DIM), lambda bi, hg, ti: (0, 0))],
        out_specs=pl.BlockSpec((None, tile, width), lambda bi, hg, ti: (bi, ti, hg)),
        out_shape=jax.ShapeDtypeStruct((b, t, heads * HEAD_DIM), MXU_DTYPE),
        scratch_shapes=[pltpu.VMEM((group, HEAD_DIM, HEAD_DIM), F32)],
        compiler_params=_cparams("parallel", "parallel", "arbitrary"),
        name="hgrn2",
    )(proj, proj, proj, proj, lb_logits.astype(F32), o_norm_g.reshape(1, HEAD_DIM))


def _cross_attn_kernel(x_ref, g_ref, wq_ref, kv_ref, wo_ref, o_ref):
    x = x_ref[...]
    xn = _rms(x, g_ref[...])
    q = _mm(xn, wq_ref[...])
    width = X_HEADS * HEAD_DIM
    scale = HEAD_DIM ** -0.5
    outs = []
    for h in range(X_HEADS):
        lo = h * HEAD_DIM
        s = _mm_nt(q[:, lo:lo + HEAD_DIM], kv_ref[:, lo:lo + HEAD_DIM]) * scale
        p = jnp.exp(s - jnp.max(s, axis=-1, keepdims=True))
        p = p / jnp.sum(p, axis=-1, keepdims=True)
        outs.append(_mm(p, kv_ref[:, width + lo:width + lo + HEAD_DIM]))
    o_ref[...] = x + _mm(jnp.concatenate(outs, axis=-1), wo_ref[...])


def _cross_attn(x, g, wq, kv, wo, *, layer, seq, tm):
    m, d = x.shape
    mem_tokens = kv.shape[0] // (m // seq)
    tm = min(tm, seq)
    assert seq % tm == 0
    per_seq = seq // tm
    width = X_HEADS * HEAD_DIM
    return pl.pallas_call(
        _cross_attn_kernel,
        grid=(m // tm,),
        in_specs=[pl.BlockSpec((tm, d), lambda i: (i, 0)),
                  pl.BlockSpec((1, d), lambda i: (0, 0)),
                  pl.BlockSpec((d, width), lambda i: (0, 0)),
                  pl.BlockSpec((mem_tokens, 2 * width), lambda i: (i // per_seq, layer)),
                  pl.BlockSpec((width, d), lambda i: (0, 0))],
        out_specs=pl.BlockSpec((tm, d), lambda i: (i, 0)),
        out_shape=jax.ShapeDtypeStruct((m, d), F32),
        compiler_params=_cparams("parallel"),
        name="cross_attn",
    )(x, g.reshape(1, d), wq, kv, wo)


TM = 512
TM_PROJ = 1024
TN = 1024
TN_OUT = 2048
TF = 768


def _moba_deltanet_mixer(x, g, w_in, conv_w, a_log, dt_bias, o_norm_g, w_out, *, batch):
    m, d = x.shape
    seq = m // batch
    dn_heads = a_log.shape[0]
    main = w_in.shape[1] - 2 * dn_heads
    dn_width = dn_heads * HEAD_DIM
    moba_heads = (main - 4 * dn_width) // (3 * HEAD_DIM)
    proj = _norm_matmul(x, g, w_in[:, :main].astype(MXU_DTYPE), tm=TM_PROJ, tn=TN)
    w_small = jnp.pad(w_in[:, main:], ((0, 0), (0, HEAD_DIM - 2 * dn_heads))).astype(MXU_DTYPE)
    small = _norm_matmul(x, g, w_small, tm=TM, tn=HEAD_DIM)
    proj = proj.reshape(batch, seq, main)
    small = small.reshape(batch, seq, HEAD_DIM)
    y_a = _moba(proj, heads=moba_heads)
    y_b = _deltanet(proj, small, conv_w, a_log, dt_bias, o_norm_g, col0=3 * moba_heads, heads=dn_heads)
    wa = w_out[:moba_heads * HEAD_DIM].astype(MXU_DTYPE)
    wb = w_out[moba_heads * HEAD_DIM:].astype(MXU_DTYPE)
    return _matmul_residual(x, [y_a.reshape(m, -1), y_b.reshape(m, -1)], [wa, wb], tm=TM, tn=TN_OUT)


def _hgrn2_mixer(x, g, w_in, lb_logits, o_norm_g, w_out, *, batch, layer):
    m, d = x.shape
    seq = m // batch
    width = w_in.shape[1] // 4
    proj = _norm_matmul(x, g, w_in.astype(MXU_DTYPE), tm=TM_PROJ, tn=TN).reshape(batch, seq, 4 * width)
    y = _hgrn2(proj, lb_logits, o_norm_g, layer=layer, heads=width // HEAD_DIM)
    return _matmul_residual(x, [y.reshape(m, width)], [w_out.astype(MXU_DTYPE)], tm=TM, tn=TN_OUT)


def kernel(x, mem, norm_g, mem_norm_g, final_norm_g, ffn_w_in, ffn_w_out, ab_w_in, ab_conv_w, ab_a_log, ab_dt_bias, ab_o_norm_g, ab_w_out, c_w_in, c_lb_logits, c_o_norm_g, c_w_out, x_w_q, x_w_kv, x_w_o):
    batch, seq, d = x.shape
    depth = norm_g.shape[0]
    m = batch * seq
    xf = x.reshape(m, d).astype(F32)
    w_kv = jnp.transpose(x_w_kv, (1, 0, 2)).reshape(d, -1).astype(MXU_DTYPE)
    kv = _norm_matmul(mem.reshape(-1, d).astype(F32), mem_norm_g, w_kv, tm=TM, tn=TN)
    ffn_weights = _ffn_prep(ffn_w_in, ffn_w_out, TF)
    for l in range(depth):
        xf = _ffn(xf, norm_g[l, 0], ffn_weights, layer=l, slot=0, tm=TM, tf=TF)
        if l % 2 == 0:
            e = l // 2
            xf = _moba_deltanet_mixer(xf, norm_g[l, 1], ab_w_in[e], ab_conv_w[e], ab_a_log[e], ab_dt_bias[e],
                                      ab_o_norm_g[e], ab_w_out[e], batch=batch)
        else:
            o = l // 2
            xf = _hgrn2_mixer(xf, norm_g[l, 1], c_w_in[o], c_lb_logits, c_o_norm_g[o], c_w_out[o],
                              batch=batch, layer=l)
        xf = _cross_attn(xf, norm_g[l, 2], x_w_q[l].astype(MXU_DTYPE), kv, x_w_o[l].astype(MXU_DTYPE),
                         layer=l, seq=seq, tm=TM)
        xf = _ffn(xf, norm_g[l, 3], ffn_weights, layer=l, slot=1, tm=TM, tf=TF,
                  final_g=final_norm_g if l == depth - 1 else None)
    return xf.reshape(batch, seq, d)
```
